```python
import math
import jax
import jax.numpy as jnp
from jax import lax
import numpy as np

D_MODEL = 1024
BATCH = 2
SEQ = 8192
DEPTH = 2
DEC_BATCH = 32
DEC_SEQ = 1
PAST_LEN = 8192
PAGE_SIZE = 128

N_MIXERS = 2
N_ATTN_LAYERS = (DEPTH + N_MIXERS - 1) // N_MIXERS
N_SGU_LAYERS = DEPTH // N_MIXERS
MIX_WIDTH = D_MODEL
MEM_LEN = 256
MEM_HEADS = 4
MEM_HEAD_DIM = 64
MEM_WIDTH = MEM_HEADS * MEM_HEAD_DIM
SELF_WIDTH = MIX_WIDTH - MEM_WIDTH
DIFF_HEAD_DIM = 64
DIFF_HEADS = SELF_WIDTH // (2 * DIFF_HEAD_DIM)
CHUNK = 128
SGU_GROUPS = 6
SGU_GROUP_DIM = SELF_WIDTH // SGU_GROUPS
D_FF = 4 * D_MODEL
Q_BLOCK = 128
ATTN_IN = 3 * SELF_WIDTH + MEM_WIDTH
SGU_IN = 2 * SELF_WIDTH + MEM_WIDTH
ALPHA = (2 * DEPTH) ** 0.25
BETA = (8 * DEPTH) ** -0.25
LN_EPS = 1e-5
NEG_INF = -1e30

kernel_name = 'diffattn_sgu_hybrid_step'


def layer_norm(x, g, b):
    xf = x.astype(jnp.float32)
    mu = jnp.mean(xf, axis=-1, keepdims=True)
    var = jnp.mean(jnp.square(xf - mu), axis=-1, keepdims=True)
    y = (xf - mu) * lax.rsqrt(var + LN_EPS)
    return (y * g.astype(jnp.float32) + b.astype(jnp.float32)).astype(x.dtype)


def post_norm(x, delta, g, b):
    return layer_norm(ALPHA * x + delta, g, b)


def lambda_init(layer):
    return 0.8 - 0.6 * math.exp(-0.3 * layer)


def diff_lambda(lqk, lam0):
    l = lqk.astype(jnp.float32)
    return jnp.exp(jnp.sum(l[0] * l[1])) - jnp.exp(jnp.sum(l[2] * l[3])) + lam0


def diff_scores(q, k):
    return jnp.einsum('bqhcd,bkhcd->bhcqk', q, k, preferred_element_type=jnp.float32) * (DIFF_HEAD_DIM ** -0.5)


def diff_weights(s, lam):
    p = jax.nn.softmax(s, axis=-1)
    return p[:, :, 0] - lam * p[:, :, 1]


def diff_post(o, g, lam0):
    of = o.astype(jnp.float32)
    y = of * lax.rsqrt(jnp.mean(jnp.square(of), axis=-1, keepdims=True) + LN_EPS)
    y = y * g.astype(jnp.float32) * (1.0 - lam0)
    return y.reshape(o.shape[0], o.shape[1], SELF_WIDTH).astype(o.dtype)


def mem_kv(mem, w):
    b, m = mem.shape[0], mem.shape[1]
    kv = mem @ w
    mk = kv[..., :MEM_WIDTH].reshape(b, m, MEM_HEADS, MEM_HEAD_DIM)
    mv = kv[..., MEM_WIDTH:].reshape(b, m, MEM_HEADS, MEM_HEAD_DIM)
    return mk, mv


def mem_attend(q, mk, mv):
    s = jnp.einsum('bqhd,bkhd->bhqk', q, mk, preferred_element_type=jnp.float32) * (MEM_HEAD_DIM ** -0.5)
    p = jax.nn.softmax(s, axis=-1).astype(mv.dtype)
    o = jnp.einsum('bhqk,bkhd->bqhd', p, mv)
    return o.reshape(q.shape[0], q.shape[1], MEM_WIDTH)


def attn_proj(h, w_in):
    b, t = h.shape[0], h.shape[1]
    p = h @ w_in
    q = p[..., :SELF_WIDTH].reshape(b, t, DIFF_HEADS, 2, DIFF_HEAD_DIM)
    k = p[..., SELF_WIDTH:2 * SELF_WIDTH].reshape(b, t, DIFF_HEADS, 2, DIFF_HEAD_DIM)
    v = p[..., 2 * SELF_WIDTH:3 * SELF_WIDTH].reshape(b, t, DIFF_HEADS, 2 * DIFF_HEAD_DIM)
    qm = p[..., 3 * SELF_WIDTH:].reshape(b, t, MEM_HEADS, MEM_HEAD_DIM)
    return q, k, v, qm


def sgu_proj(h, w_in, g, b):
    bs, t = h.shape[0], h.shape[1]
    p = h @ w_in
    uv = jax.nn.gelu(p[..., :2 * SELF_WIDTH])
    u = uv[..., :SELF_WIDTH]
    v = layer_norm(uv[..., SELF_WIDTH:], g, b).reshape(bs, t, SGU_GROUPS, SGU_GROUP_DIM)
    qm = p[..., 2 * SELF_WIDTH:].reshape(bs, t, MEM_HEADS, MEM_HEAD_DIM)
    return u, v, qm


def sgu_mask(w_s):
    return jnp.where(jnp.tril(jnp.ones((CHUNK, CHUNK), dtype=bool)), w_s, 0)


def ffn(x, w_up, w_down):
    h = jax.nn.relu(x @ w_up)
    return (h * h) @ w_down


def setup_inputs(seed: int = 0) -> dict:
    key = jax.random.key(seed)
    ks = jax.random.split(key, 32)
    n_pages = PAST_LEN // PAGE_SIZE
    n_phys = (DEC_BATCH * n_pages * 5 + 3) // 4

    def nrm(k, shape, scale=1.0):
        return jax.random.normal(k, shape, jnp.float32) * scale

    page_table = jax.random.permutation(ks[6], n_phys)[:DEC_BATCH * n_pages].reshape(DEC_BATCH, n_pages).astype(jnp.int32)
    kv_shape = (N_ATTN_LAYERS, n_phys, PAGE_SIZE, DIFF_HEADS, 2 * DIFF_HEAD_DIM)
    mem_shape = (DEPTH, DEC_BATCH, MEM_LEN, MEM_HEADS, MEM_HEAD_DIM)
    return {
        'x_prompt': nrm(ks[0], (BATCH, SEQ, D_MODEL)),
        'x_sample': nrm(ks[1], (DEC_BATCH, DEC_SEQ, D_MODEL)),
        'cache_k': nrm(ks[2], kv_shape),
        'cache_v': nrm(ks[3], kv_shape),
        'cache_mem_k': nrm(ks[4], mem_shape),
        'cache_mem_v': nrm(ks[5], mem_shape),
        'page_table': page_table,
        'mem_prompt': nrm(ks[7], (BATCH, MEM_LEN, D_MODEL)),
        'w_in_attn': nrm(ks[8], (N_ATTN_LAYERS, D_MODEL, ATTN_IN), D_MODEL ** -0.5),
        'lambda_qk': nrm(ks[9], (N_ATTN_LAYERS, 4, DIFF_HEAD_DIM), 0.1),
        'subln_g': 1.0 + nrm(ks[10], (N_ATTN_LAYERS, 2 * DIFF_HEAD_DIM), 0.02),
        'w_in_sgu': nrm(ks[11], (N_SGU_LAYERS, D_MODEL, SGU_IN), D_MODEL ** -0.5),
        'sgu_ln_g': 1.0 + nrm(ks[12], (N_SGU_LAYERS, SELF_WIDTH), 0.02),
        'sgu_ln_b': nrm(ks[13], (N_SGU_LAYERS, SELF_WIDTH), 0.02),
        'sgu_w': nrm(ks[14], (N_SGU_LAYERS, SGU_GROUPS, CHUNK, CHUNK), CHUNK ** -0.5),
        'sgu_b': 1.0 + nrm(ks[15], (N_SGU_LAYERS, SGU_GROUPS, CHUNK), 0.02),
        'w_mem_kv': nrm(ks[16], (DEPTH, D_MODEL, 2 * MEM_WIDTH), D_MODEL ** -0.5),
        'w_out': nrm(ks[17], (DEPTH, MIX_WIDTH, D_MODEL), BETA * MIX_WIDTH ** -0.5),
        'ln1_g': 1.0 + nrm(ks[18], (DEPTH, D_MODEL), 0.02),
        'ln1_b': nrm(ks[19], (DEPTH, D_MODEL), 0.02),
        'w_up': nrm(ks[20], (DEPTH, D_MODEL, D_FF), D_MODEL ** -0.5),
        'w_down': nrm(ks[21], (DEPTH, D_FF, D_MODEL), BETA * D_FF ** -0.5),
        'ln2_g': 1.0 + nrm(ks[22], (DEPTH, D_MODEL), 0.02),
        'ln2_b': nrm(ks[23], (DEPTH, D_MODEL), 0.02),
    }


def reference(x_prompt, x_sample, cache_k, cache_v, cache_mem_k, cache_mem_v, page_table, mem_prompt,
              w_in_attn, lambda_qk, subln_g, w_in_sgu, sgu_ln_g, sgu_ln_b, sgu_w, sgu_b,
              w_mem_kv, w_out, ln1_g, ln1_b, w_up, w_down, ln2_g, ln2_b):
    B, S = x_prompt.shape[0], x_prompt.shape[1]
    DB, DS = x_sample.shape[0], x_sample.shape[1]
    n_blocks = S // Q_BLOCK
    pos_p = jnp.arange(S, dtype=jnp.int32)
    pos_s = PAST_LEN + jnp.arange(DS, dtype=jnp.int32)
    xp, xs = x_prompt, x_sample
    k_p_l, v_p_l, mk_p_l, mv_p_l, k_s_l, v_s_l, sv_s_l = [], [], [], [], [], [], []

    for l in range(DEPTH):
        mk_p, mv_p = mem_kv(mem_prompt, w_mem_kv[l])
        mk_s, mv_s = cache_mem_k[l], cache_mem_v[l]
        if l % N_MIXERS == 0:
            a = l // N_MIXERS
            lam0 = lambda_init(l)
            lam = diff_lambda(lambda_qk[a], lam0)
            q, k, v, qm = attn_proj(xp, w_in_attn[a])
            qb = jnp.swapaxes(q.reshape(B, n_blocks, Q_BLOCK, DIFF_HEADS, 2, DIFF_HEAD_DIM), 0, 1)
            starts = jnp.arange(n_blocks, dtype=jnp.int32) * Q_BLOCK

            def block_attend(args, k=k, v=v, lam=lam):
                qi, s0 = args
                qpos = s0 + jnp.arange(Q_BLOCK, dtype=jnp.int32)
                s = jnp.where(pos_p[None, :] <= qpos[:, None], diff_scores(qi, k), NEG_INF)
                w = diff_weights(s, lam).astype(v.dtype)
                return jnp.einsum('bhqk,bkhe->bqhe', w, v)

            o = lax.map(block_attend, (qb, starts))
            o = jnp.swapaxes(o, 0, 1).reshape(B, S, DIFF_HEADS, 2 * DIFF_HEAD_DIM)
            mix_p = jnp.concatenate([diff_post(o, subln_g[a], lam0), mem_attend(qm, mk_p, mv_p)], axis=-1)
            k_p_l.append(k.reshape(B, S // PAGE_SIZE, PAGE_SIZE, DIFF_HEADS, 2 * DIFF_HEAD_DIM))
            v_p_l.append(v.reshape(B, S // PAGE_SIZE, PAGE_SIZE, DIFF_HEADS, 2 * DIFF_HEAD_DIM))
            q, k, v, qm = attn_proj(xs, w_in_attn[a])
            k_past = cache_k[a, page_table].reshape(DB, PAST_LEN, DIFF_HEADS, 2, DIFF_HEAD_DIM)
            v_past = cache_v[a, page_table].reshape(DB, PAST_LEN, DIFF_HEADS, 2 * DIFF_HEAD_DIM)
            s_new = jnp.where(pos_s[None, :] <= pos_s[:, None], diff_scores(q, k), NEG_INF)
            s = jnp.concatenate([diff_scores(q, k_past), s_new], axis=-1)
            w = diff_weights(s, lam).astype(v.dtype)
            o = (jnp.einsum('bhqk,bkhe->bqhe', w[..., :PAST_LEN], v_past)
                 + jnp.einsum('bhqk,bkhe->bqhe', w[..., PAST_LEN:], v))
            mix_s = jnp.concatenate([diff_post(o, subln_g[a], lam0), mem_attend(qm, mk_s, mv_s)], axis=-1)
            k_s_l.append(k.reshape(DB, DS, DIFF_HEADS, 2 * DIFF_HEAD_DIM))
            v_s_l.append(v)
        else:
            g = l // N_MIXERS
            wm = sgu_mask(sgu_w[g])
            bias = sgu_b[g].T
            u, v, qm = sgu_proj(xp, w_in_sgu[g], sgu_ln_g[g], sgu_ln_b[g])
            vc = v.reshape(B, S // CHUNK, CHUNK, SGU_GROUPS, SGU_GROUP_DIM)
            z = jnp.einsum('gts,bnsgc->bntgc', wm, vc) + bias[:, :, None]
            mix_p = jnp.concatenate([u * z.reshape(B, S, SELF_WIDTH), mem_attend(qm, mk_p, mv_p)], axis=-1)
            u, v, qm = sgu_proj(xs, w_in_sgu[g], sgu_ln_g[g], sgu_ln_b[g])
            z = jnp.einsum('gts,bsgc->btgc', wm[:, :DS, :DS], v) + bias[:DS, :, None]
            mix_s = jnp.concatenate([u * z.reshape(DB, DS, SELF_WIDTH), mem_attend(qm, mk_s, mv_s)], axis=-1)
            sv_s_l.append(v)
        mk_p_l.append(mk_p)
        mv_p_l.append(mv_p)
        xp = post_norm(xp, mix_p @ w_out[l], ln1_g[l], ln1_b[l])
        xp = post_norm(xp, ffn(xp, w_up[l], w_down[l]), ln2_g[l], ln2_b[l])
        xs = post_norm(xs, mix_s @ w_out[l], ln1_g[l], ln1_b[l])
        xs = post_norm(xs, ffn(xs, w_up[l], w_down[l]), ln2_g[l], ln2_b[l])

    return (xp, xs, jnp.stack(k_p_l), jnp.stack(v_p_l), jnp.stack(mk_p_l), jnp.stack(mv_p_l),
            jnp.stack(k_s_l), jnp.stack(v_s_l), jnp.stack(sv_s_l))
```

```python
import functools
import math

import jax
import jax.numpy as jnp
from jax import lax
from jax.experimental import pallas as pl
from jax.experimental.pallas import tpu as pltpu

F32 = jnp.float32
BF16 = jnp.bfloat16

D_MODEL = 1024
DEPTH = 2
PAGE_SIZE = 128
MEM_LEN = 256
MEM_HEADS = 4
MEM_HEAD_DIM = 64
MEM_WIDTH = MEM_HEADS * MEM_HEAD_DIM
SELF_WIDTH = D_MODEL - MEM_WIDTH
DIFF_HEAD_DIM = 64
DIFF_HEADS = SELF_WIDTH // (2 * DIFF_HEAD_DIM)
HEAD_WIDTH = 2 * DIFF_HEAD_DIM
CHUNK = 128
SGU_GROUPS = 6
D_FF = 4 * D_MODEL
ALPHA = (2 * DEPTH) ** 0.25
LN_EPS = 1e-5
NEG_INF = -1e30
QK_SCALE = DIFF_HEAD_DIM ** -0.5
MEM_SCALE = MEM_HEAD_DIM ** -0.5

LANES = 128
SUBLANES = 8
VMEM_LIMIT_BYTES = 56 * 1024 * 1024

TOKEN_TILE = 512
ATTN_TQ = 512
ATTN_TK = 512
FF_CHUNK = 1024
DECODE_PAGES_PER_STEP = 8

_NT = (((1,), (1,)), ((), ()))


def _lambda_init(layer):
    return 0.8 - 0.6 * math.exp(-0.3 * layer)


def _params(*sem):
    return pltpu.CompilerParams(dimension_semantics=sem, vmem_limit_bytes=VMEM_LIMIT_BYTES)


def _ln(x, g, b):
    mu = jnp.mean(x, axis=-1, keepdims=True)
    xc = x - mu
    var = jnp.mean(xc * xc, axis=-1, keepdims=True)
    return xc * lax.rsqrt(var + LN_EPS) * g + b


def _gelu_tanh(x):
    c = math.sqrt(2.0 / math.pi)
    return x * (0.5 * (1.0 + jnp.tanh(c * (x + 0.044715 * (x * x * x)))))


def _diff_lambda(lam_ref, lam0):
    l = lam_ref[...]
    a = jnp.sum(l[0:1, :] * l[1:2, :], axis=1, keepdims=True)
    b = jnp.sum(l[2:3, :] * l[3:4, :], axis=1, keepdims=True)
    return jnp.exp(a) - jnp.exp(b) + lam0


def _mem_kv_body(mem_ref, w_ref, mk_ref, mv_ref, mkb_ref, mvm_ref):
    kv = jnp.dot(mem_ref[...].astype(BF16), w_ref[...], preferred_element_type=F32)
    mk = kv[:, :MEM_WIDTH]
    mv = kv[:, MEM_WIDTH:]
    mk_ref[...] = mk.T
    mv_ref[...] = mv.T
    mkb_ref[...] = mk.astype(BF16)
    lane = lax.broadcasted_iota(jnp.int32, mv.shape, 1)
    for h in range(MEM_HEADS):
        mvm_ref[h * MEM_LEN:(h + 1) * MEM_LEN, :] = jnp.where(
            lane // MEM_HEAD_DIM == h, mv, 0.0).astype(BF16)


def _mem_kv(mem_prompt, w_mem_kv_bf):
    nb = mem_prompt.shape[0]
    per_lb = lambda l, b: (l, b, 0, 0)
    return pl.pallas_call(
        _mem_kv_body,
        grid=(DEPTH, nb),
        in_specs=[
            pl.BlockSpec((None, MEM_LEN, D_MODEL), lambda l, b: (b, 0, 0)),
            pl.BlockSpec((None, D_MODEL, 2 * MEM_WIDTH), lambda l, b: (l, 0, 0)),
        ],
        out_specs=[
            pl.BlockSpec((None, None, MEM_WIDTH, MEM_LEN), per_lb),
            pl.BlockSpec((None, None, MEM_WIDTH, MEM_LEN), per_lb),
            pl.BlockSpec((None, None, MEM_LEN, MEM_WIDTH), per_lb),
            pl.BlockSpec((None, None, MEM_HEADS * MEM_LEN, MEM_WIDTH), per_lb),
        ],
        out_shape=[
            jax.ShapeDtypeStruct((DEPTH, nb, MEM_WIDTH, MEM_LEN), F32),
            jax.ShapeDtypeStruct((DEPTH, nb, MEM_WIDTH, MEM_LEN), F32),
            jax.ShapeDtypeStruct((DEPTH, nb, MEM_LEN, MEM_WIDTH), BF16),
            jax.ShapeDtypeStruct((DEPTH, nb, MEM_HEADS * MEM_LEN, MEM_WIDTH), BF16),
        ],
        compiler_params=_params("arbitrary", "arbitrary"),
        name="mem_kv",
    )(mem_prompt, w_mem_kv_bf)


def _mem_attend_shared(qm, mk_ref, mvm_ref):
    lane = lax.broadcasted_iota(jnp.int32, qm.shape, 1)
    mk = mk_ref[...]
    ps = []
    for h in range(MEM_HEADS):
        qh = jnp.where(lane // MEM_HEAD_DIM == h, qm, 0.0).astype(BF16)
        s = lax.dot_general(qh, mk, _NT, preferred_element_type=F32) * MEM_SCALE
        m = jnp.max(s, axis=1, keepdims=True)
        e = jnp.exp(s - m)
        l = jnp.sum(e, axis=1, keepdims=True)
        ps.append((e * (1.0 / l)).astype(BF16))
    p = jnp.concatenate(ps, axis=1)
    return jnp.dot(p, mvm_ref[...], preferred_element_type=F32)


def _attn_proj_body(x_ref, w_ref, *refs, with_mem):
    if with_mem:
        mk_ref, mvm_ref, q_ref, kf_ref, vf_ref, kb_ref, vb_ref, m_ref = refs
    else:
        q_ref, kf_ref, vf_ref, m_ref = refs
    xb = x_ref[...].astype(BF16)
    sw = SELF_WIDTH
    q = jnp.dot(xb, w_ref[:, 0:sw], preferred_element_type=F32)
    k = jnp.dot(xb, w_ref[:, sw:2 * sw], preferred_element_type=F32)
    v = jnp.dot(xb, w_ref[:, 2 * sw:3 * sw], preferred_element_type=F32)
    qm = jnp.dot(xb, w_ref[:, 3 * sw:], preferred_element_type=F32)
    if with_mem:
        for p in range(k.shape[0] // PAGE_SIZE):
            rows = slice(p * PAGE_SIZE, (p + 1) * PAGE_SIZE)
            for h in range(DIFF_HEADS):
                cols = slice(h * HEAD_WIDTH, (h + 1) * HEAD_WIDTH)
                kf_ref[p, h] = k[rows, cols]
                vf_ref[p, h] = v[rows, cols]
        q_ref[...] = (q * QK_SCALE).astype(BF16)
        kb_ref[...] = k.astype(BF16)
        vb_ref[...] = v.astype(BF16)
        m_ref[...] = _mem_attend_shared(qm, mk_ref, mvm_ref).astype(BF16)
    else:
        kf_ref[...] = k
        vf_ref[...] = v
        q_ref[...] = q
        m_ref[...] = qm


def _attn_proj_prompt(x, w_bf, mk_bf, mvm_bf, layer, seq):
    m = x.shape[0]
    tm = TOKEN_TILE
    per_b = seq // tm
    pages = tm // PAGE_SIZE
    row = lambda i: (i, 0)
    paged = pl.BlockSpec((pages, DIFF_HEADS, PAGE_SIZE, HEAD_WIDTH), lambda i: (i, 0, 0, 0))
    paged_shape = jax.ShapeDtypeStruct((m // PAGE_SIZE, DIFF_HEADS, PAGE_SIZE, HEAD_WIDTH), F32)
    return pl.pallas_call(
        functools.partial(_attn_proj_body, with_mem=True),
        grid=(m // tm,),
        in_specs=[
            pl.BlockSpec((tm, D_MODEL), row),
            pl.BlockSpec(w_bf.shape, lambda i: (0, 0)),
            pl.BlockSpec((None, None, MEM_LEN, MEM_WIDTH), lambda i: (layer, i // per_b, 0, 0)),
            pl.BlockSpec((None, None, MEM_HEADS * MEM_LEN, MEM_WIDTH), lambda i: (layer, i // per_b, 0, 0)),
        ],
        out_specs=[
            pl.BlockSpec((tm, SELF_WIDTH), row),
            paged,
            paged,
            pl.BlockSpec((tm, SELF_WIDTH), row),
            pl.BlockSpec((tm, SELF_WIDTH), row),
            pl.BlockSpec((tm, MEM_WIDTH), row),
        ],
        out_shape=[
            jax.ShapeDtypeStruct((m, SELF_WIDTH), BF16),
            paged_shape,
            paged_shape,
            jax.ShapeDtypeStruct((m, SELF_WIDTH), BF16),
            jax.ShapeDtypeStruct((m, SELF_WIDTH), BF16),
            jax.ShapeDtypeStruct((m, MEM_WIDTH), BF16),
        ],
        compiler_params=_params("arbitrary"),
        name="attn_proj_prompt",
    )(x, w_bf, mk_bf, mvm_bf)


def _attn_proj_sample(x, w_bf):
    m = x.shape[0]
    row = lambda i: (0, 0)
    return pl.pallas_call(
        functools.partial(_attn_proj_body, with_mem=False),
        grid=(1,),
        in_specs=[pl.BlockSpec((m, D_MODEL), row), pl.BlockSpec(w_bf.shape, row)],
        out_specs=[
            pl.BlockSpec((m, SELF_WIDTH), row),
            pl.BlockSpec((m, SELF_WIDTH), row),
            pl.BlockSpec((m, SELF_WIDTH), row),
            pl.BlockSpec((m, MEM_WIDTH), row),
        ],
        out_shape=[
            jax.ShapeDtypeStruct((m, SELF_WIDTH), F32),
            jax.ShapeDtypeStruct((m, SELF_WIDTH), F32),
            jax.ShapeDtypeStruct((m, SELF_WIDTH), F32),
            jax.ShapeDtypeStruct((m, MEM_WIDTH), F32),
        ],
        compiler_params=_params("arbitrary"),
        name="attn_proj_sample",
    )(x, w_bf)


def _diff_attn_body(qi_ref, kj_ref, fl_ref, q_ref, k_ref, v_ref, lam_ref, g_ref, o_ref,
                    qs_ref, m_ref, l_ref, acc_ref, *, tq, tk, lam0):
    t = pl.program_id(2)
    qi = qi_ref[t]
    kj = kj_ref[t]
    flag = fl_ref[t]

    @pl.when(kj == 0)
    def _init():
        q = q_ref[...]
        lane = lax.broadcasted_iota(jnp.int32, q.shape, 1)
        zero = jnp.zeros_like(q)
        qs_ref[0] = jnp.where(lane < DIFF_HEAD_DIM, q, zero)
        qs_ref[1] = jnp.where(lane >= DIFF_HEAD_DIM, q, zero)
        m_ref[...] = jnp.full(m_ref.shape, NEG_INF, F32)
        l_ref[...] = jnp.zeros(l_ref.shape, F32)
        acc_ref[...] = jnp.zeros(acc_ref.shape, F32)

    def _step(masked):
        k = k_ref[...]
        v = v_ref[...]
        if masked:
            row = qi * tq + lax.broadcasted_iota(jnp.int32, (tq, tk), 0)
            col = kj * tk + lax.broadcasted_iota(jnp.int32, (tq, tk), 1)
            keep = col <= row
        for c in range(2):
            s = lax.dot_general(qs_ref[c], k, _NT, preferred_element_type=F32)
            if masked:
                s = jnp.where(keep, s, NEG_INF)
            m_prev = m_ref[c]
            m_new = jnp.maximum(m_prev, jnp.max(s, axis=1, keepdims=True))
            alpha = jnp.exp(m_prev - m_new)
            p = jnp.exp(s - pltpu.repeat(m_new, tk // LANES, axis=1))
            l_ref[c] = alpha * l_ref[c] + jnp.sum(p, axis=1, keepdims=True)
            acc_ref[c] = alpha * acc_ref[c] + jnp.dot(p.astype(BF16), v, preferred_element_type=F32)
            m_ref[c] = m_new

    @pl.when((flag & 1) == 0)
    def _plain():
        _step(False)

    @pl.when((flag & 1) == 1)
    def _diag():
        _step(True)

    @pl.when((flag & 2) != 0)
    def _finish():
        lam = _diff_lambda(lam_ref, lam0)
        o = acc_ref[0] * (1.0 / l_ref[0]) - lam * (acc_ref[1] * (1.0 / l_ref[1]))
        ms = jnp.mean(o * o, axis=1, keepdims=True)
        y = o * lax.rsqrt(ms + LN_EPS) * g_ref[...] * (1.0 - lam0)
        o_ref[...] = y.astype(o_ref.dtype)


def _attn_schedule(seq, tq, tk):
    qi, kj, fl = [], [], []
    for i in range(seq // tq):
        last_row = i * tq + tq - 1
        n = last_row // tk + 1
        for j in range(n):
            crosses = (j + 1) * tk - 1 > i * tq
            qi.append(i)
            kj.append(j)
            fl.append((1 if crosses else 0) | (2 if j == n - 1 else 0))
    to = lambda a: jnp.asarray(a, jnp.int32)
    return to(qi), to(kj), to(fl)


def _diff_attn_prompt(q_bf, k_bf, v_bf, lam_qk, g, lam0):
    nb, seq, _ = q_bf.shape
    tq, tk = ATTN_TQ, ATTN_TK
    qi, kj, fl = _attn_schedule(seq, tq, tk)
    grid_spec = pltpu.PrefetchScalarGridSpec(
        num_scalar_prefetch=3,
        grid=(nb, DIFF_HEADS, int(qi.shape[0])),
        in_specs=[
            pl.BlockSpec((None, tq, HEAD_WIDTH), lambda b, h, t, qi, kj, fl: (b, qi[t], h)),
            pl.BlockSpec((None, tk, HEAD_WIDTH), lambda b, h, t, qi, kj, fl: (b, kj[t], h)),
            pl.BlockSpec((None, tk, HEAD_WIDTH), lambda b, h, t, qi, kj, fl: (b, kj[t], h)),
            pl.BlockSpec(lam_qk.shape, lambda b, h, t, qi, kj, fl: (0, 0)),
            pl.BlockSpec(g.shape, lambda b, h, t, qi, kj, fl: (0, 0)),
        ],
        out_specs=pl.BlockSpec((None, tq, HEAD_WIDTH), lambda b, h, t, qi, kj, fl: (b, qi[t], h)),
        scratch_shapes=[
            pltpu.VMEM((2, tq, HEAD_WIDTH), BF16),
            pltpu.VMEM((2, tq, LANES), F32),
            pltpu.VMEM((2, tq, LANES), F32),
            pltpu.VMEM((2, tq, HEAD_WIDTH), F32),
        ],
    )
    return pl.pallas_call(
        functools.partial(_diff_attn_body, tq=tq, tk=tk, lam0=lam0),
        grid_spec=grid_spec,
        out_shape=jax.ShapeDtypeStruct((nb, seq, SELF_WIDTH), BF16),
        compiler_params=_params("arbitrary", "arbitrary", "arbitrary"),
        name="diff_attn_prompt",
    )(qi, kj, fl, q_bf, k_bf, v_bf, lam_qk, g)


def _diff_attn_decode_body(pt_ref, q_ref, kn_ref, vn_ref, lam_ref, g_ref, *refs,
                           pages, lam0):
    k_refs = refs[:pages]
    v_refs = refs[pages:2 * pages]
    o_ref, qr_ref, m_ref, l_ref, acc_ref = refs[2 * pages:]
    j = pl.program_id(1)
    last = j == pl.num_programs(1) - 1
    lane = lax.broadcasted_iota(jnp.int32, (SUBLANES, HEAD_WIDTH), 1)
    row = lax.broadcasted_iota(jnp.int32, (SUBLANES, HEAD_WIDTH), 0)
    row1 = lax.broadcasted_iota(jnp.int32, (SUBLANES, 1), 0)

    @pl.when(j == 0)
    def _init():
        for h in range(DIFF_HEADS):
            qh = q_ref[:, h * HEAD_WIDTH:(h + 1) * HEAD_WIDTH] * QK_SCALE
            qr_ref[h] = jnp.where(lane // DIFF_HEAD_DIM == row, qh, 0.0)
        m_ref[...] = jnp.full(m_ref.shape, NEG_INF, F32)
        l_ref[...] = jnp.zeros(l_ref.shape, F32)
        acc_ref[...] = jnp.zeros(acc_ref.shape, F32)

    for h in range(DIFF_HEADS):
        qh = qr_ref[h]
        s = jnp.concatenate(
            [lax.dot_general(qh, k_refs[p][h], _NT, preferred_element_type=F32)
             for p in range(pages)], axis=1)
        m_prev = m_ref[h][:, 0:1]
        m_new = jnp.maximum(m_prev, jnp.max(s, axis=1, keepdims=True))
        alpha = jnp.exp(m_prev - m_new)
        e = jnp.exp(s - m_new)
        l_new = alpha * l_ref[h][:, 0:1] + jnp.sum(e, axis=1, keepdims=True)
        acc = alpha * acc_ref[h]
        for p in range(pages):
            acc = acc + jnp.dot(e[:, p * PAGE_SIZE:(p + 1) * PAGE_SIZE], v_refs[p][h],
                                preferred_element_type=F32)
        acc_ref[h] = acc
        m_ref[h] = jnp.broadcast_to(m_new, (SUBLANES, LANES))
        l_ref[h] = jnp.broadcast_to(l_new, (SUBLANES, LANES))

    @pl.when(last)
    def _finish():
        lam = _diff_lambda(lam_ref, lam0)
        scale = g_ref[...] * (1.0 - lam0)
        for h in range(DIFF_HEADS):
            cols = slice(h * HEAD_WIDTH, (h + 1) * HEAD_WIDTH)
            m_old = m_ref[h][:, 0:1]
            s_new = jnp.sum(qr_ref[h] * kn_ref[:, cols], axis=1, keepdims=True)
            m_fin = jnp.maximum(m_old, s_new)
            a_fin = jnp.exp(m_old - m_fin)
            p_new = jnp.exp(s_new - m_fin)
            l_fin = a_fin * l_ref[h][:, 0:1] + p_new
            acc_fin = a_fin * acc_ref[h] + p_new * vn_ref[:, cols]
            coef = jnp.where(row1 == 0, 1.0, -lam) / l_fin
            coef = jnp.where(row1 < 2, coef, 0.0)
            oh = jnp.sum(acc_fin * coef, axis=0, keepdims=True)
            ms = jnp.mean(oh * oh, axis=1, keepdims=True)
            o_ref[:, cols] = oh * lax.rsqrt(ms + LN_EPS) * scale


def _diff_attn_decode(q, k_new, v_new, cache_k, cache_v, page_table, lam_qk, g, lam0):
    db = q.shape[0]
    n_pages = page_table.shape[1]
    pages = DECODE_PAGES_PER_STEP
    assert n_pages % pages == 0
    q3 = q.reshape(db, 1, SELF_WIDTH)
    kn3 = k_new.reshape(db, 1, SELF_WIDTH)
    vn3 = v_new.reshape(db, 1, SELF_WIDTH)
    pt = page_table.reshape(-1)
    vec = pl.BlockSpec((None, 1, SELF_WIDTH), lambda b, j, pt: (b, 0, 0))

    def page_spec(p):
        return pl.BlockSpec(
            (None, DIFF_HEADS, PAGE_SIZE, HEAD_WIDTH),
            lambda b, j, pt: (pt[b * n_pages + j * pages + p], 0, 0, 0))

    stat = pltpu.VMEM((DIFF_HEADS, SUBLANES, LANES), F32)
    grid_spec = pltpu.PrefetchScalarGridSpec(
        num_scalar_prefetch=1,
        grid=(db, n_pages // pages),
        in_specs=[vec, vec, vec,
                  pl.BlockSpec(lam_qk.shape, lambda b, j, pt: (0, 0)),
                  pl.BlockSpec(g.shape, lambda b, j, pt: (0, 0))]
                 + [page_spec(p) for p in range(pages)]
                 + [page_spec(p) for p in range(pages)],
        out_specs=vec,
        scratch_shapes=[pltpu.VMEM((DIFF_HEADS, SUBLANES, HEAD_WIDTH), F32), stat, stat,
                        pltpu.VMEM((DIFF_HEADS, SUBLANES, HEAD_WIDTH), F32)],
    )
    out = pl.pallas_call(
        functools.partial(_diff_attn_decode_body, pages=pages, lam0=lam0),
        grid_spec=grid_spec,
        out_shape=jax.ShapeDtypeStruct((db, 1, SELF_WIDTH), F32),
        compiler_params=_params("arbitrary", "arbitrary"),
        name="diff_attn_decode",
    )(pt, q3, kn3, vn3, lam_qk, g, *([cache_k] * pages), *([cache_v] * pages))
    return out.reshape(db, SELF_WIDTH)


def _mem_decode_body(qm_ref, mkt_ref, mvt_ref, o_ref):
    lane = lax.broadcasted_iota(jnp.int32, (SUBLANES, MEM_WIDTH), 1)
    row = lax.broadcasted_iota(jnp.int32, (SUBLANES, MEM_WIDTH), 0)
    own = lane // MEM_HEAD_DIM == row
    qr = jnp.where(own, qm_ref[...], 0.0)
    s = jnp.dot(qr, mkt_ref[...], preferred_element_type=F32) * MEM_SCALE
    m = jnp.max(s, axis=1, keepdims=True)
    e = jnp.exp(s - m)
    p = e * (1.0 / jnp.sum(e, axis=1, keepdims=True))
    o = lax.dot_general(p, mvt_ref[...], _NT, preferred_element_type=F32)
    o_ref[...] = jnp.sum(jnp.where(own, o, 0.0), axis=0, keepdims=True)


def _mem_decode(qm, mem_kt, mem_vt, layer):
    db = qm.shape[0]
    vec = pl.BlockSpec((None, 1, MEM_WIDTH), lambda b: (b, 0, 0))
    mem = pl.BlockSpec((None, None, MEM_WIDTH, MEM_LEN), lambda b: (layer, b, 0, 0))
    out = pl.pallas_call(
        _mem_decode_body,
        grid=(db,),
        in_specs=[vec, mem, mem],
        out_specs=vec,
        out_shape=jax.ShapeDtypeStruct((db, 1, MEM_WIDTH), F32),
        compiler_params=_params("arbitrary"),
        name="mem_decode",
    )(qm.reshape(db, 1, MEM_WIDTH), mem_kt, mem_vt)
    return out.reshape(db, MEM_WIDTH)


def _sgu_body(x_ref, w_ref, lng_ref, lnb_ref, *refs, prompt):
    if prompt:
        ws_ref, sb_ref, mk_ref, mvm_ref, a_ref, m_ref = refs
    else:
        sc_ref, sh_ref, a_ref, m_ref, sv_ref = refs
    xb = x_ref[...].astype(BF16)
    sw = SELF_WIDTH
    u = _gelu_tanh(jnp.dot(xb, w_ref[:, 0:sw], preferred_element_type=F32))
    vv = _gelu_tanh(jnp.dot(xb, w_ref[:, sw:2 * sw], preferred_element_type=F32))
    qm = jnp.dot(xb, w_ref[:, 2 * sw:], preferred_element_type=F32)
    v = _ln(vv, lng_ref[...], lnb_ref[...])
    if prompt:
        vb = v.astype(BF16)
        tm = vb.shape[0]
        r = lax.broadcasted_iota(jnp.int32, (CHUNK, CHUNK), 0)
        c = lax.broadcasted_iota(jnp.int32, (CHUNK, CHUNK), 1)
        for g in range(SGU_GROUPS):
            wm = jnp.where(r >= c, ws_ref[g], 0.0).astype(BF16)
            cols = slice(g * CHUNK, (g + 1) * CHUNK)
            for n in range(tm // CHUNK):
                rows = slice(n * CHUNK, (n + 1) * CHUNK)
                z = jnp.dot(wm, vb[rows, cols], preferred_element_type=F32) + sb_ref[g]
                a_ref[rows, cols] = (u[rows, cols] * z).astype(BF16)
        m_ref[...] = _mem_attend_shared(qm, mk_ref, mvm_ref).astype(BF16)
    else:
        sv_ref[...] = v
        a_ref[...] = u * (sc_ref[...] * v + sh_ref[...])
        m_ref[...] = qm


def _sgu_prompt(x, w_bf, ln_g, ln_b, sgu_w, sgu_b_full, mk_bf, mvm_bf, layer, seq):
    m = x.shape[0]
    tm = TOKEN_TILE
    per_b = seq // tm
    row = lambda i: (i, 0)
    const2 = lambda i: (0, 0)
    const3 = lambda i: (0, 0, 0)
    return pl.pallas_call(
        functools.partial(_sgu_body, prompt=True),
        grid=(m // tm,),
        in_specs=[
            pl.BlockSpec((tm, D_MODEL), row),
            pl.BlockSpec(w_bf.shape, const2),
            pl.BlockSpec(ln_g.shape, const2),
            pl.BlockSpec(ln_b.shape, const2),
            pl.BlockSpec(sgu_w.shape, const3),
            pl.BlockSpec(sgu_b_full.shape, const3),
            pl.BlockSpec((None, None, MEM_LEN, MEM_WIDTH), lambda i: (layer, i // per_b, 0, 0)),
            pl.BlockSpec((None, None, MEM_HEADS * MEM_LEN, MEM_WIDTH), lambda i: (layer, i // per_b, 0, 0)),
        ],
        out_specs=[pl.BlockSpec((tm, SELF_WIDTH), row), pl.BlockSpec((tm, MEM_WIDTH), row)],
        out_shape=[jax.ShapeDtypeStruct((m, SELF_WIDTH), BF16),
                   jax.ShapeDtypeStruct((m, MEM_WIDTH), BF16)],
        compiler_params=_params("arbitrary"),
        name="sgu_prompt",
    )(x, w_bf, ln_g, ln_b, sgu_w, sgu_b_full, mk_bf, mvm_bf)


def _sgu_sample(x, w_bf, ln_g, ln_b, scale_row, shift_row):
    m = x.shape[0]
    c = lambda i: (0, 0)
    return pl.pallas_call(
        functools.partial(_sgu_body, prompt=False),
        grid=(1,),
        in_specs=[
            pl.BlockSpec((m, D_MODEL), c),
            pl.BlockSpec(w_bf.shape, c),
            pl.BlockSpec(ln_g.shape, c),
            pl.BlockSpec(ln_b.shape, c),
            pl.BlockSpec(scale_row.shape, c),
            pl.BlockSpec(shift_row.shape, c),
        ],
        out_specs=[pl.BlockSpec((m, SELF_WIDTH), c), pl.BlockSpec((m, MEM_WIDTH), c),
                   pl.BlockSpec((m, SELF_WIDTH), c)],
        out_shape=[jax.ShapeDtypeStruct((m, SELF_WIDTH), F32),
                   jax.ShapeDtypeStruct((m, MEM_WIDTH), F32),
                   jax.ShapeDtypeStruct((m, SELF_WIDTH), F32)],
        compiler_params=_params("arbitrary"),
        name="sgu_sample",
    )(x, w_bf, ln_g, ln_b, scale_row, shift_row)


def _out_ffn_body(x_ref, a_ref, m_ref, wo_ref, g1_ref, b1_ref, wu_ref, wd_ref, g2_ref, b2_ref,
                  o_ref):
    x = x_ref[...]
    d = jnp.dot(a_ref[...].astype(BF16), wo_ref[0:SELF_WIDTH, :], preferred_element_type=F32)
    d = d + jnp.dot(m_ref[...].astype(BF16), wo_ref[SELF_WIDTH:, :], preferred_element_type=F32)
    y = _ln(ALPHA * x + d, g1_ref[...], b1_ref[...])
    yb = y.astype(BF16)
    acc = jnp.zeros_like(y)
    for c in range(D_FF // FF_CHUNK):
        cols = slice(c * FF_CHUNK, (c + 1) * FF_CHUNK)
        h = jnp.maximum(jnp.dot(yb, wu_ref[:, cols], preferred_element_type=F32), 0.0)
        acc = acc + jnp.dot((h * h).astype(BF16), wd_ref[cols, :], preferred_element_type=F32)
    o_ref[...] = _ln(ALPHA * y + acc, g2_ref[...], b2_ref[...])


def _out_ffn(x, a, ma, wo_bf, g1, b1, wu_bf, wd_bf, g2, b2, tm, name):
    m = x.shape[0]
    row = lambda i: (i, 0)
    c = lambda i: (0, 0)
    resident = lambda arr: pl.BlockSpec(arr.shape, c, pipeline_mode=pl.Buffered(1))
    return pl.pallas_call(
        _out_ffn_body,
        grid=(m // tm,),
        in_specs=[
            pl.BlockSpec((tm, D_MODEL), row),
            pl.BlockSpec((tm, SELF_WIDTH), row),
            pl.BlockSpec((tm, MEM_WIDTH), row),
            resident(wo_bf), resident(g1), resident(b1),
            resident(wu_bf), resident(wd_bf), resident(g2), resident(b2),
        ],
        out_specs=pl.BlockSpec((tm, D_MODEL), row),
        out_shape=jax.ShapeDtypeStruct((m, D_MODEL), F32),
        compiler_params=_params("arbitrary"),
        name=name,
    )(x, a, ma, wo_bf, g1, b1, wu_bf, wd_bf, g2, b2)


def kernel(x_prompt, x_sample, cache_k, cache_v, cache_mem_k, cache_mem_v, page_table, mem_prompt, w_in_attn, lambda_qk, subln_g, w_in_sgu, sgu_ln_g, sgu_ln_b, sgu_w, sgu_b, w_mem_kv, w_out, ln1_g, ln1_b, w_up, w_down, ln2_g, ln2_b):
    nb, seq, _ = x_prompt.shape
    db = x_sample.shape[0]
    row = lambda a: a.reshape(1, -1)

    xp = x_prompt.reshape(nb * seq, D_MODEL)
    xs = x_sample.reshape(db, D_MODEL)
    w_in_attn_bf = w_in_attn.astype(BF16)
    w_in_sgu_bf = w_in_sgu.astype(BF16)
    w_out_bf = w_out.astype(BF16)
    w_up_bf = w_up.astype(BF16)
    w_down_bf = w_down.astype(BF16)
    cache_k_pg = jnp.transpose(cache_k[0], (0, 2, 1, 3))
    cache_v_pg = jnp.transpose(cache_v[0], (0, 2, 1, 3))
    mem_kt = jnp.transpose(cache_mem_k, (0, 1, 3, 4, 2)).reshape(DEPTH, db, MEM_WIDTH, MEM_LEN)
    mem_vt = jnp.transpose(cache_mem_v, (0, 1, 3, 4, 2)).reshape(DEPTH, db, MEM_WIDTH, MEM_LEN)

    mkt_p, mvt_p, mk_bf, mvm_bf = _mem_kv(mem_prompt, w_mem_kv.astype(BF16))

    def out_ffn(x, a, ma, l, tm, name):
        return _out_ffn(x, a, ma, w_out_bf[l], row(ln1_g[l]), row(ln1_b[l]), w_up_bf[l],
                        w_down_bf[l], row(ln2_g[l]), row(ln2_b[l]), tm, name)

    lam0 = _lambda_init(0)
    g0 = row(subln_g[0])
    q_bf, k_pg, v_pg, k_bf, v_bf, ma_p = _attn_proj_prompt(xp, w_in_attn_bf[0], mk_bf, mvm_bf, 0, seq)
    o_p = _diff_attn_prompt(q_bf.reshape(nb, seq, SELF_WIDTH), k_bf.reshape(nb, seq, SELF_WIDTH),
                            v_bf.reshape(nb, seq, SELF_WIDTH), lambda_qk[0], g0, lam0)
    xp = out_ffn(xp, o_p.reshape(nb * seq, SELF_WIDTH), ma_p, 0, TOKEN_TILE, "out_ffn_prompt0")

    q_s, k_s, v_s, qm_s = _attn_proj_sample(xs, w_in_attn_bf[0])
    o_s = _diff_attn_decode(q_s, k_s, v_s, cache_k_pg, cache_v_pg, page_table, lambda_qk[0], g0, lam0)
    ma_s = _mem_decode(qm_s, mem_kt, mem_vt, 0)
    xs = out_ffn(xs, o_s, ma_s, 0, db, "out_ffn_sample0")

    sgu_b_full = jnp.broadcast_to(sgu_b[0][:, :, None], (SGU_GROUPS, CHUNK, CHUNK))
    a_p, ma_p = _sgu_prompt(xp, w_in_sgu_bf[0], row(sgu_ln_g[0]), row(sgu_ln_b[0]), sgu_w[0],
                            sgu_b_full, mk_bf, mvm_bf, 1, seq)
    xp = out_ffn(xp, a_p, ma_p, 1, TOKEN_TILE, "out_ffn_prompt1")

    scale_row = row(jnp.repeat(sgu_w[0][:, 0, 0], CHUNK))
    shift_row = row(jnp.repeat(sgu_b[0][:, 0], CHUNK))
    a_s, qm_s, sv_s = _sgu_sample(xs, w_in_sgu_bf[0], row(sgu_ln_g[0]), row(sgu_ln_b[0]),
                                  scale_row, shift_row)
    ma_s = _mem_decode(qm_s, mem_kt, mem_vt, 1)
    xs = out_ffn(xs, a_s, ma_s, 1, db, "out_ffn_sample1")

    n_pg = seq // PAGE_SIZE
    paged = lambda a: jnp.transpose(
        a.reshape(1, nb, n_pg, DIFF_HEADS, PAGE_SIZE, HEAD_WIDTH), (0, 1, 2, 4, 3, 5))
    mem_out = lambda a: jnp.transpose(
        a.reshape(DEPTH, nb, MEM_HEADS, MEM_HEAD_DIM, MEM_LEN), (0, 1, 4, 2, 3))
    dec_shape = (1, db, 1, DIFF_HEADS, HEAD_WIDTH)
    return (xp.reshape(nb, seq, D_MODEL), xs.reshape(db, 1, D_MODEL),
            paged(k_pg), paged(v_pg), mem_out(mkt_p), mem_out(mvt_p),
            k_s.reshape(dec_shape), v_s.reshape(dec_shape), sv_s.reshape(dec_shape))
```

```python
import functools
import math

import jax
import jax.numpy as jnp
from jax import lax
from jax.experimental import pallas as pl
from jax.experimental.pallas import tpu as pltpu

F32 = jnp.float32
BF16 = jnp.bfloat16

D_MODEL = 1024
DEPTH = 2
PAGE_SIZE = 128
MEM_LEN = 256
MEM_HEADS = 4
MEM_HEAD_DIM = 64
MEM_WIDTH = MEM_HEADS * MEM_HEAD_DIM
SELF_WIDTH = D_MODEL - MEM_WIDTH
DIFF_HEAD_DIM = 64
DIFF_HEADS = SELF_WIDTH // (2 * DIFF_HEAD_DIM)
HEAD_WIDTH = 2 * DIFF_HEAD_DIM
CHUNK = 128
SGU_GROUPS = 6
D_FF = 4 * D_MODEL
ALPHA = (2 * DEPTH) ** 0.25
LN_EPS = 1e-5
NEG_INF = -1e30
QK_SCALE = DIFF_HEAD_DIM ** -0.5
QK_SCALE_LOG2 = QK_SCALE * math.log2(math.e)
MEM_SCALE = MEM_HEAD_DIM ** -0.5

LANES = 128
SUBLANES = 8
BF16_SUBLANES = 16
VT_ROWS = HEAD_WIDTH + BF16_SUBLANES
VMEM_LIMIT_BYTES = 56 * 1024 * 1024

TOKEN_TILE = 512
ATTN_TQ = 1024
ATTN_TK = 1024
ATTN_SUB_Q = 512
ATTN_SCORES_AHEAD = 2
FF_CHUNK = 1024
DECODE_PAGES_PER_STEP = 8

_NT = (((1,), (1,)), ((), ()))


def _lambda_init(layer):
    return 0.8 - 0.6 * math.exp(-0.3 * layer)


def _params(*sem):
    return pltpu.CompilerParams(dimension_semantics=sem, vmem_limit_bytes=VMEM_LIMIT_BYTES)


def _ln(x, g, b):
    mu = jnp.mean(x, axis=-1, keepdims=True)
    xc = x - mu
    var = jnp.mean(xc * xc, axis=-1, keepdims=True)
    return xc * lax.rsqrt(var + LN_EPS) * g + b


def _gelu_tanh(x):
    c = math.sqrt(2.0 / math.pi)
    return x * (0.5 * (1.0 + jnp.tanh(c * (x + 0.044715 * (x * x * x)))))


def _diff_lambda(lam_ref, lam0):
    l = lam_ref[...]
    a = jnp.sum(l[0:1, :] * l[1:2, :], axis=1, keepdims=True)
    b = jnp.sum(l[2:3, :] * l[3:4, :], axis=1, keepdims=True)
    return jnp.exp(a) - jnp.exp(b) + lam0


def _mem_kv_body(mem_ref, w_ref, mk_ref, mv_ref, mkb_ref, mvm_ref):
    kv = jnp.dot(mem_ref[...].astype(BF16), w_ref[...], preferred_element_type=F32)
    mk = kv[:, :MEM_WIDTH]
    mv = kv[:, MEM_WIDTH:]
    mk_ref[...] = mk.T
    mv_ref[...] = mv.T
    mkb_ref[...] = mk.astype(BF16)
    lane = lax.broadcasted_iota(jnp.int32, mv.shape, 1)
    for h in range(MEM_HEADS):
        mvm_ref[h * MEM_LEN:(h + 1) * MEM_LEN, :] = jnp.where(
            lane // MEM_HEAD_DIM == h, mv, 0.0).astype(BF16)


def _mem_kv(mem_prompt, w_mem_kv_bf):
    nb = mem_prompt.shape[0]
    per_lb = lambda l, b: (l, b, 0, 0)
    return pl.pallas_call(
        _mem_kv_body,
        grid=(DEPTH, nb),
        in_specs=[
            pl.BlockSpec((None, MEM_LEN, D_MODEL), lambda l, b: (b, 0, 0)),
            pl.BlockSpec((None, D_MODEL, 2 * MEM_WIDTH), lambda l, b: (l, 0, 0)),
        ],
        out_specs=[
            pl.BlockSpec((None, None, MEM_WIDTH, MEM_LEN), per_lb),
            pl.BlockSpec((None, None, MEM_WIDTH, MEM_LEN), per_lb),
            pl.BlockSpec((None, None, MEM_LEN, MEM_WIDTH), per_lb),
            pl.BlockSpec((None, None, MEM_HEADS * MEM_LEN, MEM_WIDTH), per_lb),
        ],
        out_shape=[
            jax.ShapeDtypeStruct((DEPTH, nb, MEM_WIDTH, MEM_LEN), F32),
            jax.ShapeDtypeStruct((DEPTH, nb, MEM_WIDTH, MEM_LEN), F32),
            jax.ShapeDtypeStruct((DEPTH, nb, MEM_LEN, MEM_WIDTH), BF16),
            jax.ShapeDtypeStruct((DEPTH, nb, MEM_HEADS * MEM_LEN, MEM_WIDTH), BF16),
        ],
        compiler_params=_params("arbitrary", "arbitrary"),
        name="mem_kv",
    )(mem_prompt, w_mem_kv_bf)


def _mem_attend_shared(qm, mk_ref, mvm_ref):
    lane = lax.broadcasted_iota(jnp.int32, qm.shape, 1)
    mk = mk_ref[...]
    ps = []
    for h in range(MEM_HEADS):
        qh = jnp.where(lane // MEM_HEAD_DIM == h, qm, 0.0).astype(BF16)
        s = lax.dot_general(qh, mk, _NT, preferred_element_type=F32) * MEM_SCALE
        m = jnp.max(s, axis=1, keepdims=True)
        e = jnp.exp(s - m)
        l = jnp.sum(e, axis=1, keepdims=True)
        ps.append((e * (1.0 / l)).astype(BF16))
    p = jnp.concatenate(ps, axis=1)
    return jnp.dot(p, mvm_ref[...], preferred_element_type=F32)


def _attn_proj_body(x_ref, w_ref, *refs, with_mem):
    if with_mem:
        mk_ref, mvm_ref, q_ref, kf_ref, vf_ref, kb_ref, vt_ref, m_ref = refs
    else:
        q_ref, kf_ref, vf_ref, m_ref = refs
    xb = x_ref[...].astype(BF16)
    sw = SELF_WIDTH
    q = jnp.dot(xb, w_ref[:, 0:sw], preferred_element_type=F32)
    k = jnp.dot(xb, w_ref[:, sw:2 * sw], preferred_element_type=F32)
    v = jnp.dot(xb, w_ref[:, 2 * sw:3 * sw], preferred_element_type=F32)
    qm = jnp.dot(xb, w_ref[:, 3 * sw:], preferred_element_type=F32)
    if with_mem:
        for h in range(DIFF_HEADS):
            cols = slice(h * HEAD_WIDTH, (h + 1) * HEAD_WIDTH)
            for p in range(k.shape[0] // PAGE_SIZE):
                rows = slice(p * PAGE_SIZE, (p + 1) * PAGE_SIZE)
                kf_ref[p, h] = k[rows, cols]
                vf_ref[p, h] = v[rows, cols]
            q_ref[h] = (q[:, cols] * QK_SCALE_LOG2).astype(BF16)
            kb_ref[h] = k[:, cols].astype(BF16)
            vt_ref[h, 0:HEAD_WIDTH, :] = v[:, cols].T.astype(BF16)
            vt_ref[h, HEAD_WIDTH:, :] = jnp.ones((BF16_SUBLANES, v.shape[0]), BF16)
        m_ref[...] = _mem_attend_shared(qm, mk_ref, mvm_ref).astype(BF16)
    else:
        kf_ref[...] = k
        vf_ref[...] = v
        q_ref[...] = q
        m_ref[...] = qm


def _attn_proj_prompt(x, w_bf, mk_bf, mvm_bf, layer, seq):
    m = x.shape[0]
    tm = TOKEN_TILE
    per_b = seq // tm
    pages = tm // PAGE_SIZE
    row = lambda i: (i, 0)
    nb = m // seq
    paged = pl.BlockSpec((pages, DIFF_HEADS, PAGE_SIZE, HEAD_WIDTH), lambda i: (i, 0, 0, 0))
    paged_shape = jax.ShapeDtypeStruct((m // PAGE_SIZE, DIFF_HEADS, PAGE_SIZE, HEAD_WIDTH), F32)
    heads = pl.BlockSpec((None, DIFF_HEADS, tm, HEAD_WIDTH), lambda i: (i // per_b, 0, i % per_b, 0))
    heads_shape = jax.ShapeDtypeStruct((nb, DIFF_HEADS, seq, HEAD_WIDTH), BF16)
    return pl.pallas_call(
        functools.partial(_attn_proj_body, with_mem=True),
        grid=(m // tm,),
        in_specs=[
            pl.BlockSpec((tm, D_MODEL), row),
            pl.BlockSpec(w_bf.shape, lambda i: (0, 0)),
            pl.BlockSpec((None, None, MEM_LEN, MEM_WIDTH), lambda i: (layer, i // per_b, 0, 0)),
            pl.BlockSpec((None, None, MEM_HEADS * MEM_LEN, MEM_WIDTH), lambda i: (layer, i // per_b, 0, 0)),
        ],
        out_specs=[
            heads,
            paged,
            paged,
            heads,
            pl.BlockSpec((None, DIFF_HEADS, VT_ROWS, tm), lambda i: (i // per_b, 0, 0, i % per_b)),
            pl.BlockSpec((tm, MEM_WIDTH), row),
        ],
        out_shape=[
            heads_shape,
            paged_shape,
            paged_shape,
            heads_shape,
            jax.ShapeDtypeStruct((nb, DIFF_HEADS, VT_ROWS, seq), BF16),
            jax.ShapeDtypeStruct((m, MEM_WIDTH), BF16),
        ],
        compiler_params=_params("arbitrary"),
        name="attn_proj_prompt",
    )(x, w_bf, mk_bf, mvm_bf)


def _attn_proj_sample(x, w_bf):
    m = x.shape[0]
    row = lambda i: (0, 0)
    return pl.pallas_call(
        functools.partial(_attn_proj_body, with_mem=False),
        grid=(1,),
        in_specs=[pl.BlockSpec((m, D_MODEL), row), pl.BlockSpec(w_bf.shape, row)],
        out_specs=[
            pl.BlockSpec((m, SELF_WIDTH), row),
            pl.BlockSpec((m, SELF_WIDTH), row),
            pl.BlockSpec((m, SELF_WIDTH), row),
            pl.BlockSpec((m, MEM_WIDTH), row),
        ],
        out_shape=[
            jax.ShapeDtypeStruct((m, SELF_WIDTH), F32),
            jax.ShapeDtypeStruct((m, SELF_WIDTH), F32),
            jax.ShapeDtypeStruct((m, SELF_WIDTH), F32),
            jax.ShapeDtypeStruct((m, MEM_WIDTH), F32),
        ],
        compiler_params=_params("arbitrary"),
        name="attn_proj_sample",
    )(x, w_bf)


def _diff_attn_body(qi_ref, kj_ref, fl_ref, q_ref, k_ref, vt_ref, lam_ref, g_ref, o_ref,
                    qs_ref, m_ref, l_ref, acc_ref, *, tq, tk, lam0):
    t = pl.program_id(2)
    qi = qi_ref[t]
    kj = kj_ref[t]
    flag = fl_ref[t]

    @pl.when(kj == 0)
    def _init():
        q = q_ref[...]
        lane = lax.broadcasted_iota(jnp.int32, q.shape, 1)
        zero = jnp.zeros_like(q)
        qs_ref[0] = jnp.where(lane < DIFF_HEAD_DIM, q, zero)
        qs_ref[1] = jnp.where(lane >= DIFF_HEAD_DIM, q, zero)
        m_ref[...] = jnp.full(m_ref.shape, NEG_INF, F32)
        l_ref[...] = jnp.zeros(l_ref.shape, F32)
        acc_ref[...] = jnp.zeros(acc_ref.shape, F32)

    def _step(masked):
        k = k_ref[...]
        vt = vt_ref[...]
        sq = ATTN_SUB_Q
        chains = [(qb, c) for qb in range(tq // sq) for c in range(2)]

        def n_keys(qb):
            return min(tk, (qb + 1) * sq) if (masked and tq == tk) else tk

        def scores(qb, c):
            cols = slice(qb * sq, (qb + 1) * sq)
            nk = n_keys(qb)
            s = lax.dot_general(k[0:nk, :], qs_ref[c, cols, :], _NT,
                                preferred_element_type=F32)
            if masked:
                key = kj * tk + lax.broadcasted_iota(jnp.int32, (nk, sq), 0)
                qry = qi * tq + qb * sq + lax.broadcasted_iota(jnp.int32, (nk, sq), 1)
                s = jnp.where(key <= qry, s, NEG_INF)
            return s

        ahead = ATTN_SCORES_AHEAD
        pending = [scores(*ch) for ch in chains[:ahead]]
        for i, (qb, c) in enumerate(chains):
            cols = slice(qb * sq, (qb + 1) * sq)
            s = pending.pop(0)
            if i + ahead < len(chains):
                pending.append(scores(*chains[i + ahead]))
            m_prev = m_ref[c, 0:1, cols]
            m_new = jnp.maximum(m_prev, jnp.max(s, axis=0, keepdims=True))
            alpha = jnp.exp2(m_prev - m_new)
            p = jnp.exp2(s - m_new).astype(BF16)
            pv = jnp.dot(vt[:, 0:n_keys(qb)], p, preferred_element_type=F32)
            l_new = alpha * l_ref[c, 0:1, cols] + pv[HEAD_WIDTH:HEAD_WIDTH + 1, :]
            acc_ref[c, :, cols] = alpha * acc_ref[c, :, cols] + pv[0:HEAD_WIDTH, :]
            m_ref[c, :, cols] = jnp.broadcast_to(m_new, (SUBLANES, sq))
            l_ref[c, :, cols] = jnp.broadcast_to(l_new, (SUBLANES, sq))

    @pl.when((flag & 1) == 0)
    def _plain():
        _step(False)

    @pl.when((flag & 1) == 1)
    def _diag():
        _step(True)

    @pl.when((flag & 2) != 0)
    def _finish():
        lam = _diff_lambda(lam_ref, lam0)
        inv0 = 1.0 / l_ref[0, 0:1, :]
        inv1 = 1.0 / l_ref[1, 0:1, :]
        ot = acc_ref[0] * inv0 - lam * (acc_ref[1] * inv1)
        ms = jnp.mean(ot * ot, axis=0, keepdims=True)
        yt = ot * lax.rsqrt(ms + LN_EPS)
        o_ref[...] = (yt.T * (g_ref[...] * (1.0 - lam0))).astype(o_ref.dtype)


def _attn_schedule(seq, tq, tk):
    qi, kj, fl = [], [], []
    for i in range(seq // tq):
        last_row = i * tq + tq - 1
        n = last_row // tk + 1
        for j in range(n):
            crosses = (j + 1) * tk - 1 > i * tq
            qi.append(i)
            kj.append(j)
            fl.append((1 if crosses else 0) | (2 if j == n - 1 else 0))
    to = lambda a: jnp.asarray(a, jnp.int32)
    return to(qi), to(kj), to(fl)


def _diff_attn_prompt(q_bf, k_bf, vt_bf, lam_qk, g, lam0):
    nb, _, seq, _ = q_bf.shape
    tq, tk = ATTN_TQ, ATTN_TK
    qi, kj, fl = _attn_schedule(seq, tq, tk)
    grid_spec = pltpu.PrefetchScalarGridSpec(
        num_scalar_prefetch=3,
        grid=(nb, DIFF_HEADS, int(qi.shape[0])),
        in_specs=[
            pl.BlockSpec((None, None, tq, HEAD_WIDTH), lambda b, h, t, qi, kj, fl: (b, h, qi[t], 0)),
            pl.BlockSpec((None, None, tk, HEAD_WIDTH), lambda b, h, t, qi, kj, fl: (b, h, kj[t], 0)),
            pl.BlockSpec((None, None, VT_ROWS, tk), lambda b, h, t, qi, kj, fl: (b, h, 0, kj[t])),
            pl.BlockSpec(lam_qk.shape, lambda b, h, t, qi, kj, fl: (0, 0)),
            pl.BlockSpec(g.shape, lambda b, h, t, qi, kj, fl: (0, 0)),
        ],
        out_specs=pl.BlockSpec((None, tq, HEAD_WIDTH), lambda b, h, t, qi, kj, fl: (b, qi[t], h)),
        scratch_shapes=[
            pltpu.VMEM((2, tq, HEAD_WIDTH), BF16),
            pltpu.VMEM((2, SUBLANES, tq), F32),
            pltpu.VMEM((2, SUBLANES, tq), F32),
            pltpu.VMEM((2, HEAD_WIDTH, tq), F32),
        ],
    )
    return pl.pallas_call(
        functools.partial(_diff_attn_body, tq=tq, tk=tk, lam0=lam0),
        grid_spec=grid_spec,
        out_shape=jax.ShapeDtypeStruct((nb, seq, SELF_WIDTH), BF16),
        compiler_params=_params("arbitrary", "arbitrary", "arbitrary"),
        name="diff_attn_prompt",
    )(qi, kj, fl, q_bf, k_bf, vt_bf, lam_qk, g)


def _diff_attn_decode_body(pt_ref, q_ref, kn_ref, vn_ref, lam_ref, g_ref, *refs,
                           pages, lam0):
    k_refs = refs[:pages]
    v_refs = refs[pages:2 * pages]
    o_ref, qr_ref, m_ref, l_ref, acc_ref = refs[2 * pages:]
    j = pl.program_id(1)
    last = j == pl.num_programs(1) - 1
    lane = lax.broadcasted_iota(jnp.int32, (SUBLANES, HEAD_WIDTH), 1)
    row = lax.broadcasted_iota(jnp.int32, (SUBLANES, HEAD_WIDTH), 0)
    row1 = lax.broadcasted_iota(jnp.int32, (SUBLANES, 1), 0)

    def head_rows(h):
        return slice(h * SUBLANES, (h + 1) * SUBLANES)

    @pl.when(j == 0)
    def _init():
        for h in range(DIFF_HEADS):
            qh = q_ref[:, h * HEAD_WIDTH:(h + 1) * HEAD_WIDTH] * QK_SCALE
            qr_ref[head_rows(h), :] = jnp.where(lane // DIFF_HEAD_DIM == row, qh, 0.0)
        m_ref[...] = jnp.full(m_ref.shape, NEG_INF, F32)
        l_ref[...] = jnp.zeros(l_ref.shape, F32)
        acc_ref[...] = jnp.zeros(acc_ref.shape, F32)

    s = jnp.concatenate(
        [jnp.concatenate(
            [lax.dot_general(qr_ref[head_rows(h), :], k_refs[p][h], _NT,
                             preferred_element_type=F32) for p in range(pages)], axis=1)
         for h in range(DIFF_HEADS)], axis=0)
    m_prev = m_ref[:, 0:1]
    m_new = jnp.maximum(m_prev, jnp.max(s, axis=1, keepdims=True))
    alpha = jnp.exp(m_prev - m_new)
    e = jnp.exp(s - m_new)
    l_new = alpha * l_ref[:, 0:1] + jnp.sum(e, axis=1, keepdims=True)
    pv = []
    for h in range(DIFF_HEADS):
        acc_h = jnp.dot(e[head_rows(h), 0:PAGE_SIZE], v_refs[0][h], preferred_element_type=F32)
        for p in range(1, pages):
            acc_h = acc_h + jnp.dot(e[head_rows(h), p * PAGE_SIZE:(p + 1) * PAGE_SIZE],
                                    v_refs[p][h], preferred_element_type=F32)
        pv.append(acc_h)
    acc_ref[...] = alpha * acc_ref[...] + jnp.concatenate(pv, axis=0)
    m_ref[...] = jnp.broadcast_to(m_new, m_ref.shape)
    l_ref[...] = jnp.broadcast_to(l_new, l_ref.shape)

    @pl.when(last)
    def _finish():
        lam = _diff_lambda(lam_ref, lam0)
        scale = g_ref[...] * (1.0 - lam0)
        for h in range(DIFF_HEADS):
            cols = slice(h * HEAD_WIDTH, (h + 1) * HEAD_WIDTH)
            m_old = m_ref[head_rows(h), 0:1]
            s_new = jnp.sum(qr_ref[head_rows(h), :] * kn_ref[:, cols], axis=1, keepdims=True)
            m_fin = jnp.maximum(m_old, s_new)
            a_fin = jnp.exp(m_old - m_fin)
            p_new = jnp.exp(s_new - m_fin)
            l_fin = a_fin * l_ref[head_rows(h), 0:1] + p_new
            acc_fin = a_fin * acc_ref[head_rows(h), :] + p_new * vn_ref[:, cols]
            coef = jnp.where(row1 == 0, 1.0, -lam) / l_fin
            coef = jnp.where(row1 < 2, coef, 0.0)
            oh = jnp.sum(acc_fin * coef, axis=0, keepdims=True)
            ms = jnp.mean(oh * oh, axis=1, keepdims=True)
            o_ref[:, cols] = oh * lax.rsqrt(ms + LN_EPS) * scale


def _diff_attn_decode(q, k_new, v_new, cache_k, cache_v, page_table, lam_qk, g, lam0):
    db = q.shape[0]
    n_pages = page_table.shape[1]
    pages = DECODE_PAGES_PER_STEP
    assert n_pages % pages == 0
    q3 = q.reshape(db, 1, SELF_WIDTH)
    kn3 = k_new.reshape(db, 1, SELF_WIDTH)
    vn3 = v_new.reshape(db, 1, SELF_WIDTH)
    pt = page_table.reshape(-1)
    vec = pl.BlockSpec((None, 1, SELF_WIDTH), lambda b, j, pt: (b, 0, 0))

    def page_spec(p):
        return pl.BlockSpec(
            (None, DIFF_HEADS, PAGE_SIZE, HEAD_WIDTH),
            lambda b, j, pt: (pt[b * n_pages + j * pages + p], 0, 0, 0))

    stat = pltpu.VMEM((DIFF_HEADS * SUBLANES, LANES), F32)
    grid_spec = pltpu.PrefetchScalarGridSpec(
        num_scalar_prefetch=1,
        grid=(db, n_pages // pages),
        in_specs=[vec, vec, vec,
                  pl.BlockSpec(lam_qk.shape, lambda b, j, pt: (0, 0)),
                  pl.BlockSpec(g.shape, lambda b, j, pt: (0, 0))]
                 + [page_spec(p) for p in range(pages)]
                 + [page_spec(p) for p in range(pages)],
        out_specs=vec,
        scratch_shapes=[pltpu.VMEM((DIFF_HEADS * SUBLANES, HEAD_WIDTH), F32), stat, stat,
                        pltpu.VMEM((DIFF_HEADS * SUBLANES, HEAD_WIDTH), F32)],
    )
    out = pl.pallas_call(
        functools.partial(_diff_attn_decode_body, pages=pages, lam0=lam0),
        grid_spec=grid_spec,
        out_shape=jax.ShapeDtypeStruct((db, 1, SELF_WIDTH), F32),
        compiler_params=_params("arbitrary", "arbitrary"),
        name="diff_attn_decode",
    )(pt, q3, kn3, vn3, lam_qk, g, *([cache_k] * pages), *([cache_v] * pages))
    return out.reshape(db, SELF_WIDTH)


def _mem_decode_body(qm_ref, mkt_ref, mvt_ref, o_ref):
    lane = lax.broadcasted_iota(jnp.int32, (SUBLANES, MEM_WIDTH), 1)
    row = lax.broadcasted_iota(jnp.int32, (SUBLANES, MEM_WIDTH), 0)
    own = lane // MEM_HEAD_DIM == row
    qr = jnp.where(own, qm_ref[...], 0.0)
    s = jnp.dot(qr, mkt_ref[...], preferred_element_type=F32) * MEM_SCALE
    m = jnp.max(s, axis=1, keepdims=True)
    e = jnp.exp(s - m)
    p = e * (1.0 / jnp.sum(e, axis=1, keepdims=True))
    o = lax.dot_general(p, mvt_ref[...], _NT, preferred_element_type=F32)
    o_ref[...] = jnp.sum(jnp.where(own, o, 0.0), axis=0, keepdims=True)


def _mem_decode(qm, mem_kt, mem_vt, layer):
    db = qm.shape[0]
    vec = pl.BlockSpec((None, 1, MEM_WIDTH), lambda b: (b, 0, 0))
    mem = pl.BlockSpec((None, None, MEM_WIDTH, MEM_LEN), lambda b: (layer, b, 0, 0))
    out = pl.pallas_call(
        _mem_decode_body,
        grid=(db,),
        in_specs=[vec, mem, mem],
        out_specs=vec,
        out_shape=jax.ShapeDtypeStruct((db, 1, MEM_WIDTH), F32),
        compiler_params=_params("arbitrary"),
        name="mem_decode",
    )(qm.reshape(db, 1, MEM_WIDTH), mem_kt, mem_vt)
    return out.reshape(db, MEM_WIDTH)


def _sgu_body(x_ref, w_ref, lng_ref, lnb_ref, *refs, prompt):
    if prompt:
        ws_ref, sb_ref, mk_ref, mvm_ref, a_ref, m_ref = refs
    else:
        sc_ref, sh_ref, a_ref, m_ref, sv_ref = refs
    xb = x_ref[...].astype(BF16)
    sw = SELF_WIDTH
    u = _gelu_tanh(jnp.dot(xb, w_ref[:, 0:sw], preferred_element_type=F32))
    vv = _gelu_tanh(jnp.dot(xb, w_ref[:, sw:2 * sw], preferred_element_type=F32))
    qm = jnp.dot(xb, w_ref[:, 2 * sw:], preferred_element_type=F32)
    v = _ln(vv, lng_ref[...], lnb_ref[...])
    if prompt:
        vb = v.astype(BF16)
        tm = vb.shape[0]
        r = lax.broadcasted_iota(jnp.int32, (CHUNK, CHUNK), 0)
        c = lax.broadcasted_iota(jnp.int32, (CHUNK, CHUNK), 1)
        for g in range(SGU_GROUPS):
            wm = jnp.where(r >= c, ws_ref[g], 0.0).astype(BF16)
            cols = slice(g * CHUNK, (g + 1) * CHUNK)
            for n in range(tm // CHUNK):
                rows = slice(n * CHUNK, (n + 1) * CHUNK)
                z = jnp.dot(wm, vb[rows, cols], preferred_element_type=F32) + sb_ref[g]
                a_ref[rows, cols] = (u[rows, cols] * z).astype(BF16)
        m_ref[...] = _mem_attend_shared(qm, mk_ref, mvm_ref).astype(BF16)
    else:
        sv_ref[...] = v
        a_ref[...] = u * (sc_ref[...] * v + sh_ref[...])
        m_ref[...] = qm


def _sgu_prompt(x, w_bf, ln_g, ln_b, sgu_w, sgu_b_full, mk_bf, mvm_bf, layer, seq):
    m = x.shape[0]
    tm = TOKEN_TILE
    per_b = seq // tm
    row = lambda i: (i, 0)
    const2 = lambda i: (0, 0)
    const3 = lambda i: (0, 0, 0)
    return pl.pallas_call(
        functools.partial(_sgu_body, prompt=True),
        grid=(m // tm,),
        in_specs=[
            pl.BlockSpec((tm, D_MODEL), row),
            pl.BlockSpec(w_bf.shape, const2),
            pl.BlockSpec(ln_g.shape, const2),
            pl.BlockSpec(ln_b.shape, const2),
            pl.BlockSpec(sgu_w.shape, const3),
            pl.BlockSpec(sgu_b_full.shape, const3),
            pl.BlockSpec((None, None, MEM_LEN, MEM_WIDTH), lambda i: (layer, i // per_b, 0, 0)),
            pl.BlockSpec((None, None, MEM_HEADS * MEM_LEN, MEM_WIDTH), lambda i: (layer, i // per_b, 0, 0)),
        ],
        out_specs=[pl.BlockSpec((tm, SELF_WIDTH), row), pl.BlockSpec((tm, MEM_WIDTH), row)],
        out_shape=[jax.ShapeDtypeStruct((m, SELF_WIDTH), BF16),
                   jax.ShapeDtypeStruct((m, MEM_WIDTH), BF16)],
        compiler_params=_params("arbitrary"),
        name="sgu_prompt",
    )(x, w_bf, ln_g, ln_b, sgu_w, sgu_b_full, mk_bf, mvm_bf)


def _sgu_sample(x, w_bf, ln_g, ln_b, scale_row, shift_row):
    m = x.shape[0]
    c = lambda i: (0, 0)
    return pl.pallas_call(
        functools.partial(_sgu_body, prompt=False),
        grid=(1,),
        in_specs=[
            pl.BlockSpec((m, D_MODEL), c),
            pl.BlockSpec(w_bf.shape, c),
            pl.BlockSpec(ln_g.shape, c),
            pl.BlockSpec(ln_b.shape, c),
            pl.BlockSpec(scale_row.shape, c),
            pl.BlockSpec(shift_row.shape, c),
        ],
        out_specs=[pl.BlockSpec((m, SELF_WIDTH), c), pl.BlockSpec((m, MEM_WIDTH), c),
                   pl.BlockSpec((m, SELF_WIDTH), c)],
        out_shape=[jax.ShapeDtypeStruct((m, SELF_WIDTH), F32),
                   jax.ShapeDtypeStruct((m, MEM_WIDTH), F32),
                   jax.ShapeDtypeStruct((m, SELF_WIDTH), F32)],
        compiler_params=_params("arbitrary"),
        name="sgu_sample",
    )(x, w_bf, ln_g, ln_b, scale_row, shift_row)


def _out_ffn_body(x_ref, a_ref, m_ref, wo_ref, g1_ref, b1_ref, wu_ref, wd_ref, g2_ref, b2_ref,
                  o_ref):
    x = x_ref[...]
    d = jnp.dot(a_ref[...].astype(BF16), wo_ref[0:SELF_WIDTH, :], preferred_element_type=F32)
    d = d + jnp.dot(m_ref[...].astype(BF16), wo_ref[SELF_WIDTH:, :], preferred_element_type=F32)
    y = _ln(ALPHA * x + d, g1_ref[...], b1_ref[...])
    yb = y.astype(BF16)
    acc = jnp.zeros_like(y)
    for c in range(D_FF // FF_CHUNK):
        cols = slice(c * FF_CHUNK, (c + 1) * FF_CHUNK)
        h = jnp.maximum(jnp.dot(yb, wu_ref[:, cols], preferred_element_type=F32), 0.0)
        acc = acc + jnp.dot((h * h).astype(BF16), wd_ref[cols, :], preferred_element_type=F32)
    o_ref[...] = _ln(ALPHA * y + acc, g2_ref[...], b2_ref[...])


def _out_ffn(x, a, ma, wo_bf, g1, b1, wu_bf, wd_bf, g2, b2, layer, tm, name):
    m = x.shape[0]
    row = lambda i: (i, 0)
    resident = lambda arr: pl.BlockSpec((None,) + arr.shape[1:], lambda i: (layer, 0, 0),
                                        pipeline_mode=pl.Buffered(1))
    return pl.pallas_call(
        _out_ffn_body,
        grid=(m // tm,),
        in_specs=[
            pl.BlockSpec((tm, D_MODEL), row),
            pl.BlockSpec((tm, SELF_WIDTH), row),
            pl.BlockSpec((tm, MEM_WIDTH), row),
            resident(wo_bf), resident(g1), resident(b1),
            resident(wu_bf), resident(wd_bf), resident(g2), resident(b2),
        ],
        out_specs=pl.BlockSpec((tm, D_MODEL), row),
        out_shape=jax.ShapeDtypeStruct((m, D_MODEL), F32),
        compiler_params=_params("arbitrary"),
        name=name,
    )(x, a, ma, wo_bf, g1, b1, wu_bf, wd_bf, g2, b2)


def kernel(x_prompt, x_sample, cache_k, cache_v, cache_mem_k, cache_mem_v, page_table, mem_prompt, w_in_attn, lambda_qk, subln_g, w_in_sgu, sgu_ln_g, sgu_ln_b, sgu_w, sgu_b, w_mem_kv, w_out, ln1_g, ln1_b, w_up, w_down, ln2_g, ln2_b):
    nb, seq, _ = x_prompt.shape
    db = x_sample.shape[0]
    row = lambda a: a.reshape(1, -1)

    xp = x_prompt.reshape(nb * seq, D_MODEL)
    xs = x_sample.reshape(db, D_MODEL)
    w_in_attn_bf = w_in_attn.astype(BF16)
    w_in_sgu_bf = w_in_sgu.astype(BF16)
    w_out_bf = w_out.astype(BF16)
    w_up_bf = w_up.astype(BF16)
    w_down_bf = w_down.astype(BF16)
    cache_k_pg = jnp.transpose(cache_k[0], (0, 2, 1, 3))
    cache_v_pg = jnp.transpose(cache_v[0], (0, 2, 1, 3))
    mem_kt = jnp.transpose(cache_mem_k, (0, 1, 3, 4, 2)).reshape(DEPTH, db, MEM_WIDTH, MEM_LEN)
    mem_vt = jnp.transpose(cache_mem_v, (0, 1, 3, 4, 2)).reshape(DEPTH, db, MEM_WIDTH, MEM_LEN)

    mkt_p, mvt_p, mk_bf, mvm_bf = _mem_kv(mem_prompt, w_mem_kv.astype(BF16))

    ln_rows = lambda a: a.reshape(DEPTH, 1, D_MODEL)
    ln1_g3, ln1_b3, ln2_g3, ln2_b3 = ln_rows(ln1_g), ln_rows(ln1_b), ln_rows(ln2_g), ln_rows(ln2_b)

    def out_ffn(x, a, ma, l, tm, name):
        return _out_ffn(x, a, ma, w_out_bf, ln1_g3, ln1_b3, w_up_bf, w_down_bf, ln2_g3, ln2_b3,
                        l, tm, name)

    lam0 = _lambda_init(0)
    g0 = row(subln_g[0])
    q_bf, k_pg, v_pg, k_bf, vt_bf, ma_p = _attn_proj_prompt(xp, w_in_attn_bf[0], mk_bf, mvm_bf, 0, seq)
    o_p = _diff_attn_prompt(q_bf, k_bf, vt_bf, lambda_qk[0], g0, lam0)
    xp = out_ffn(xp, o_p.reshape(nb * seq, SELF_WIDTH), ma_p, 0, TOKEN_TILE, "out_ffn_prompt0")

    q_s, k_s, v_s, qm_s = _attn_proj_sample(xs, w_in_attn_bf[0])
    o_s = _diff_attn_decode(q_s, k_s, v_s, cache_k_pg, cache_v_pg, page_table, lambda_qk[0], g0, lam0)
    ma_s = _mem_decode(qm_s, mem_kt, mem_vt, 0)
    xs = out_ffn(xs, o_s, ma_s, 0, db, "out_ffn_sample0")

    sgu_b_full = jnp.broadcast_to(sgu_b[0][:, :, None], (SGU_GROUPS, CHUNK, CHUNK))
    a_p, ma_p = _sgu_prompt(xp, w_in_sgu_bf[0], row(sgu_ln_g[0]), row(sgu_ln_b[0]), sgu_w[0],
                            sgu_b_full, mk_bf, mvm_bf, 1, seq)
    xp = out_ffn(xp, a_p, ma_p, 1, TOKEN_TILE, "out_ffn_prompt1")

    scale_row = row(jnp.repeat(sgu_w[0][:, 0, 0], CHUNK))
    shift_row = row(jnp.repeat(sgu_b[0][:, 0], CHUNK))
    a_s, qm_s, sv_s = _sgu_sample(xs, w_in_sgu_bf[0], row(sgu_ln_g[0]), row(sgu_ln_b[0]),
                                  scale_row, shift_row)
    ma_s = _mem_decode(qm_s, mem_kt, mem_vt, 1)
    xs = out_ffn(xs, a_s, ma_s, 1, db, "out_ffn_sample1")

    n_pg = seq // PAGE_SIZE
    paged = lambda a: jnp.transpose(
        a.reshape(1, nb, n_pg, DIFF_HEADS, PAGE_SIZE, HEAD_WIDTH), (0, 1, 2, 4, 3, 5))
    mem_out = lambda a: jnp.transpose(
        a.reshape(DEPTH, nb, MEM_HEADS, MEM_HEAD_DIM, MEM_LEN), (0, 1, 4, 2, 3))
    dec_shape = (1, db, 1, DIFF_HEADS, HEAD_WIDTH)
    return (xp.reshape(nb, seq, D_MODEL), xs.reshape(db, 1, D_MODEL),
            paged(k_pg), paged(v_pg), mem_out(mkt_p), mem_out(mvt_p),
            k_s.reshape(dec_shape), v_s.reshape(dec_shape), sv_s.reshape(dec_shape))
```

```python
import functools
import math

import jax
import jax.numpy as jnp
from jax import lax
from jax.experimental import pallas as pl
from jax.experimental.pallas import tpu as pltpu

F32 = jnp.float32
BF16 = jnp.bfloat16

D_MODEL = 1024
DEPTH = 2
PAGE_SIZE = 128
MEM_LEN = 256
MEM_HEADS = 4
MEM_HEAD_DIM = 64
MEM_WIDTH = MEM_HEADS * MEM_HEAD_DIM
SELF_WIDTH = D_MODEL - MEM_WIDTH
DIFF_HEAD_DIM = 64
DIFF_HEADS = SELF_WIDTH // (2 * DIFF_HEAD_DIM)
HEAD_WIDTH = 2 * DIFF_HEAD_DIM
CHUNK = 128
SGU_GROUPS = 6
D_FF = 4 * D_MODEL
ALPHA = (2 * DEPTH) ** 0.25
LN_EPS = 1e-5
NEG_INF = -1e30
QK_SCALE = DIFF_HEAD_DIM ** -0.5
QK_SCALE_LOG2 = QK_SCALE * math.log2(math.e)
MEM_SCALE = MEM_HEAD_DIM ** -0.5

LANES = 128
SUBLANES = 8
BF16_SUBLANES = 16
VT_ROWS = HEAD_WIDTH + BF16_SUBLANES
VMEM_LIMIT_BYTES = 56 * 1024 * 1024

TOKEN_TILE = 512
ATTN_BLOCK = 1024
FF_CHUNK = 1024
DECODE_PAGES_PER_STEP = 16
MEM_DECODE_ROWS_PER_STEP = 8

_NT = (((1,), (1,)), ((), ()))


def _lambda_init(layer):
    return 0.8 - 0.6 * math.exp(-0.3 * layer)


def _params(*sem):
    return pltpu.CompilerParams(dimension_semantics=sem, vmem_limit_bytes=VMEM_LIMIT_BYTES)


def _ln(x, g, b):
    mu = jnp.mean(x, axis=-1, keepdims=True)
    xc = x - mu
    var = jnp.mean(xc * xc, axis=-1, keepdims=True)
    return xc * lax.rsqrt(var + LN_EPS) * g + b


def _gelu_tanh(x):
    c = math.sqrt(2.0 / math.pi)
    return x * (0.5 * (1.0 + jnp.tanh(c * (x + 0.044715 * (x * x * x)))))


def _diff_lambda(lam_ref, lam0):
    l = lam_ref[...]
    a = jnp.sum(l[0:1, :] * l[1:2, :], axis=1, keepdims=True)
    b = jnp.sum(l[2:3, :] * l[3:4, :], axis=1, keepdims=True)
    return jnp.exp(a) - jnp.exp(b) + lam0


def _mem_kv_body(mem_ref, w_ref, mk_ref, mv_ref, mkb_ref, mvm_ref):
    kv = jnp.dot(mem_ref[...].astype(BF16), w_ref[...], preferred_element_type=F32)
    mk = kv[:, :MEM_WIDTH]
    mv = kv[:, MEM_WIDTH:]
    mk_ref[...] = mk.T
    mv_ref[...] = mv.T
    mkb_ref[...] = mk.astype(BF16)
    lane = lax.broadcasted_iota(jnp.int32, mv.shape, 1)
    for h in range(MEM_HEADS):
        mvm_ref[h * MEM_LEN:(h + 1) * MEM_LEN, :] = jnp.where(
            lane // MEM_HEAD_DIM == h, mv, 0.0).astype(BF16)


def _mem_kv(mem_prompt, w_mem_kv_bf):
    nb = mem_prompt.shape[0]
    per_lb = lambda l, b: (l, b, 0, 0)
    return pl.pallas_call(
        _mem_kv_body,
        grid=(DEPTH, nb),
        in_specs=[
            pl.BlockSpec((None, MEM_LEN, D_MODEL), lambda l, b: (b, 0, 0)),
            pl.BlockSpec((None, D_MODEL, 2 * MEM_WIDTH), lambda l, b: (l, 0, 0)),
        ],
        out_specs=[
            pl.BlockSpec((None, None, MEM_WIDTH, MEM_LEN), per_lb),
            pl.BlockSpec((None, None, MEM_WIDTH, MEM_LEN), per_lb),
            pl.BlockSpec((None, None, MEM_LEN, MEM_WIDTH), per_lb),
            pl.BlockSpec((None, None, MEM_HEADS * MEM_LEN, MEM_WIDTH), per_lb),
        ],
        out_shape=[
            jax.ShapeDtypeStruct((DEPTH, nb, MEM_WIDTH, MEM_LEN), F32),
            jax.ShapeDtypeStruct((DEPTH, nb, MEM_WIDTH, MEM_LEN), F32),
            jax.ShapeDtypeStruct((DEPTH, nb, MEM_LEN, MEM_WIDTH), BF16),
            jax.ShapeDtypeStruct((DEPTH, nb, MEM_HEADS * MEM_LEN, MEM_WIDTH), BF16),
        ],
        compiler_params=_params("arbitrary", "arbitrary"),
        name="mem_kv",
    )(mem_prompt, w_mem_kv_bf)


def _mem_attend_shared(qm, mk_ref, mvm_ref):
    lane = lax.broadcasted_iota(jnp.int32, qm.shape, 1)
    mk = mk_ref[...]
    ps = []
    for h in range(MEM_HEADS):
        qh = jnp.where(lane // MEM_HEAD_DIM == h, qm, 0.0).astype(BF16)
        s = lax.dot_general(qh, mk, _NT, preferred_element_type=F32) * MEM_SCALE
        m = jnp.max(s, axis=1, keepdims=True)
        e = jnp.exp(s - m)
        l = jnp.sum(e, axis=1, keepdims=True)
        ps.append((e * (1.0 / l)).astype(BF16))
    p = jnp.concatenate(ps, axis=1)
    return jnp.dot(p, mvm_ref[...], preferred_element_type=F32)


def _attn_proj_body(x_ref, w_ref, *refs, with_mem):
    if with_mem:
        mk_ref, mvm_ref, q_ref, kf_ref, vf_ref, kb_ref, vt_ref, m_ref = refs
    else:
        q_ref, kf_ref, vf_ref, m_ref = refs
    xb = x_ref[...].astype(BF16)
    sw = SELF_WIDTH
    q = jnp.dot(xb, w_ref[:, 0:sw], preferred_element_type=F32)
    k = jnp.dot(xb, w_ref[:, sw:2 * sw], preferred_element_type=F32)
    v = jnp.dot(xb, w_ref[:, 2 * sw:3 * sw], preferred_element_type=F32)
    qm = jnp.dot(xb, w_ref[:, 3 * sw:], preferred_element_type=F32)
    if with_mem:
        for h in range(DIFF_HEADS):
            cols = slice(h * HEAD_WIDTH, (h + 1) * HEAD_WIDTH)
            for p in range(k.shape[0] // PAGE_SIZE):
                rows = slice(p * PAGE_SIZE, (p + 1) * PAGE_SIZE)
                kf_ref[p, h] = k[rows, cols]
                vf_ref[p, h] = v[rows, cols]
            q_ref[h] = (q[:, cols] * QK_SCALE_LOG2).astype(BF16)
            kb_ref[h] = k[:, cols].astype(BF16)
            vt_ref[h, 0:HEAD_WIDTH, :] = v[:, cols].T.astype(BF16)
            vt_ref[h, HEAD_WIDTH:, :] = jnp.ones((BF16_SUBLANES, v.shape[0]), BF16)
        m_ref[...] = _mem_attend_shared(qm, mk_ref, mvm_ref).astype(BF16)
    else:
        kf_ref[...] = k
        vf_ref[...] = v
        q_ref[...] = q
        m_ref[...] = qm


def _attn_proj_prompt(x, w_bf, mk_bf, mvm_bf, layer, seq):
    m = x.shape[0]
    tm = TOKEN_TILE
    per_b = seq // tm
    pages = tm // PAGE_SIZE
    row = lambda i: (i, 0)
    nb = m // seq
    per_blk = ATTN_BLOCK // tm
    paged = pl.BlockSpec((pages, DIFF_HEADS, PAGE_SIZE, HEAD_WIDTH), lambda i: (i, 0, 0, 0))
    paged_shape = jax.ShapeDtypeStruct((m // PAGE_SIZE, DIFF_HEADS, PAGE_SIZE, HEAD_WIDTH), F32)
    heads = pl.BlockSpec((None, DIFF_HEADS, tm, HEAD_WIDTH), lambda i: (i // per_b, 0, i % per_b, 0))
    heads_shape = jax.ShapeDtypeStruct((nb, DIFF_HEADS, seq, HEAD_WIDTH), BF16)
    return pl.pallas_call(
        functools.partial(_attn_proj_body, with_mem=True),
        grid=(m // tm,),
        in_specs=[
            pl.BlockSpec((tm, D_MODEL), row),
            pl.BlockSpec(w_bf.shape, lambda i: (0, 0)),
            pl.BlockSpec((None, None, MEM_LEN, MEM_WIDTH), lambda i: (layer, i // per_b, 0, 0)),
            pl.BlockSpec((None, None, MEM_HEADS * MEM_LEN, MEM_WIDTH), lambda i: (layer, i // per_b, 0, 0)),
        ],
        out_specs=[
            heads,
            paged,
            paged,
            heads,
            pl.BlockSpec((None, DIFF_HEADS, None, VT_ROWS, tm),
                         lambda i: (i // per_b, 0, (i % per_b) // per_blk, 0, (i % per_b) % per_blk)),
            pl.BlockSpec((tm, MEM_WIDTH), row),
        ],
        out_shape=[
            heads_shape,
            paged_shape,
            paged_shape,
            heads_shape,
            jax.ShapeDtypeStruct((nb, DIFF_HEADS, seq // ATTN_BLOCK, VT_ROWS, ATTN_BLOCK), BF16),
            jax.ShapeDtypeStruct((m, MEM_WIDTH), BF16),
        ],
        compiler_params=_params("arbitrary"),
        name="attn_proj_prompt",
    )(x, w_bf, mk_bf, mvm_bf)


def _attn_proj_sample(x, w_bf):
    m = x.shape[0]
    row = lambda i: (0, 0)
    return pl.pallas_call(
        functools.partial(_attn_proj_body, with_mem=False),
        grid=(1,),
        in_specs=[pl.BlockSpec((m, D_MODEL), row), pl.BlockSpec(w_bf.shape, row)],
        out_specs=[
            pl.BlockSpec((m, SELF_WIDTH), row),
            pl.BlockSpec((m, SELF_WIDTH), row),
            pl.BlockSpec((m, SELF_WIDTH), row),
            pl.BlockSpec((m, MEM_WIDTH), row),
        ],
        out_shape=[
            jax.ShapeDtypeStruct((m, SELF_WIDTH), F32),
            jax.ShapeDtypeStruct((m, SELF_WIDTH), F32),
            jax.ShapeDtypeStruct((m, SELF_WIDTH), F32),
            jax.ShapeDtypeStruct((m, MEM_WIDTH), F32),
        ],
        compiler_params=_params("arbitrary"),
        name="attn_proj_sample",
    )(x, w_bf)


def _diff_attn_body(q_ref, k_ref, vt_ref, lam_ref, g_ref, o_ref,
                    qs_ref, m_ref, l_ref, acc_ref, sbuf_ref, *, blk, lam0):
    qi = pl.program_id(2)
    sq = blk // 2
    chains = [(1, 0), (1, 1), (0, 0), (0, 1)]

    def _init():
        q = q_ref[...]
        lane = lax.broadcasted_iota(jnp.int32, q.shape, 1)
        zero = jnp.zeros_like(q)
        qs_ref[0] = jnp.where(lane < DIFF_HEAD_DIM, q, zero)
        qs_ref[1] = jnp.where(lane >= DIFF_HEAD_DIM, q, zero)
        m_ref[...] = jnp.full(m_ref.shape, NEG_INF, F32)
        l_ref[...] = jnp.zeros(l_ref.shape, F32)
        acc_ref[...] = jnp.zeros(acc_ref.shape, F32)

    def cols(qb):
        return slice(qb * sq, (qb + 1) * sq)

    def scores(j, qb, c, nk=blk):
        k = k_ref[pl.ds(pl.multiple_of(j * blk, blk), nk), :]
        return lax.dot_general(k, qs_ref[c, cols(qb), :], _NT, preferred_element_type=F32)

    def causal(s, qb):
        key = lax.broadcasted_iota(jnp.int32, s.shape, 0)
        qry = qb * sq + lax.broadcasted_iota(jnp.int32, s.shape, 1)
        return jnp.where(key <= qry, s, NEG_INF)

    def reduce_chain(s, j, qb, c):
        nk = s.shape[0]
        m_prev = m_ref[c, 0:1, cols(qb)]
        m_new = jnp.maximum(m_prev, jnp.max(s, axis=0, keepdims=True))
        alpha = jnp.exp2(m_prev - m_new)
        p = jnp.exp2(s - m_new).astype(BF16)
        pv = jnp.dot(vt_ref[j, :, 0:nk], p, preferred_element_type=F32)
        l_new = alpha * l_ref[c, 0:1, cols(qb)] + pv[HEAD_WIDTH:HEAD_WIDTH + 1, :]
        acc_ref[c, :, cols(qb)] = alpha * acc_ref[c, :, cols(qb)] + pv[0:HEAD_WIDTH, :]
        m_ref[c, :, cols(qb)] = jnp.broadcast_to(m_new, (SUBLANES, sq))
        l_ref[c, :, cols(qb)] = jnp.broadcast_to(l_new, (SUBLANES, sq))

    _init()
    sbuf_ref[0] = scores(0, *chains[0])
    sbuf_ref[1] = scores(0, *chains[1])

    def _plain_block(j, carry):
        s2 = scores(j, *chains[2])
        reduce_chain(sbuf_ref[0], j, *chains[0])
        s3 = scores(j, *chains[3])
        reduce_chain(sbuf_ref[1], j, *chains[1])
        sbuf_ref[0] = scores(j + 1, *chains[0])
        reduce_chain(s2, j, *chains[2])
        sbuf_ref[1] = scores(j + 1, *chains[1])
        reduce_chain(s3, j, *chains[3])
        return carry

    lax.fori_loop(0, qi, _plain_block, 0)

    s2 = causal(scores(qi, *chains[2], nk=sq), chains[2][0])
    reduce_chain(causal(sbuf_ref[0], chains[0][0]), qi, *chains[0])
    s3 = causal(scores(qi, *chains[3], nk=sq), chains[3][0])
    reduce_chain(causal(sbuf_ref[1], chains[1][0]), qi, *chains[1])
    reduce_chain(s2, qi, *chains[2])
    reduce_chain(s3, qi, *chains[3])

    lam = _diff_lambda(lam_ref, lam0)
    inv0 = 1.0 / l_ref[0, 0:1, :]
    inv1 = 1.0 / l_ref[1, 0:1, :]
    ot = acc_ref[0] * inv0 - lam * (acc_ref[1] * inv1)
    ms = jnp.mean(ot * ot, axis=0, keepdims=True)
    yt = ot * lax.rsqrt(ms + LN_EPS)
    o_ref[...] = (yt.T * (g_ref[...] * (1.0 - lam0))).astype(o_ref.dtype)


def _diff_attn_prompt(q_bf, k_bf, vt_bf, lam_qk, g, lam0):
    nb, _, seq, _ = q_bf.shape
    blk = ATTN_BLOCK
    n_blk = seq // blk
    per_head = lambda b, h, i: (b, h, 0, 0)
    return pl.pallas_call(
        functools.partial(_diff_attn_body, blk=blk, lam0=lam0),
        grid=(nb, DIFF_HEADS, n_blk),
        in_specs=[
            pl.BlockSpec((None, None, blk, HEAD_WIDTH), lambda b, h, i: (b, h, i, 0)),
            pl.BlockSpec((None, None, seq, HEAD_WIDTH), per_head),
            pl.BlockSpec((None, None, n_blk, VT_ROWS, blk), lambda b, h, i: (b, h, 0, 0, 0)),
            pl.BlockSpec(lam_qk.shape, lambda b, h, i: (0, 0)),
            pl.BlockSpec(g.shape, lambda b, h, i: (0, 0)),
        ],
        out_specs=pl.BlockSpec((None, blk, HEAD_WIDTH), lambda b, h, i: (b, i, h)),
        out_shape=jax.ShapeDtypeStruct((nb, seq, SELF_WIDTH), BF16),
        scratch_shapes=[
            pltpu.VMEM((2, blk, HEAD_WIDTH), BF16),
            pltpu.VMEM((2, SUBLANES, blk), F32),
            pltpu.VMEM((2, SUBLANES, blk), F32),
            pltpu.VMEM((2, HEAD_WIDTH, blk), F32),
            pltpu.VMEM((2, blk, blk // 2), F32),
        ],
        compiler_params=_params("arbitrary", "arbitrary", "arbitrary"),
        name="diff_attn_prompt",
    )(q_bf, k_bf, vt_bf, lam_qk, g)


def _diff_attn_decode_body(pt_ref, q_ref, kn_ref, vn_ref, lam_ref, g_ref, *refs,
                           pages, lam0):
    k_refs = refs[:pages]
    v_refs = refs[pages:2 * pages]
    o_ref, qr_ref, m_ref, l_ref, acc_ref = refs[2 * pages:]
    j = pl.program_id(1)
    last = j == pl.num_programs(1) - 1
    lane = lax.broadcasted_iota(jnp.int32, (SUBLANES, HEAD_WIDTH), 1)
    row = lax.broadcasted_iota(jnp.int32, (SUBLANES, HEAD_WIDTH), 0)
    row1 = lax.broadcasted_iota(jnp.int32, (SUBLANES, 1), 0)

    def head_rows(h):
        return slice(h * SUBLANES, (h + 1) * SUBLANES)

    @pl.when(j == 0)
    def _init():
        for h in range(DIFF_HEADS):
            qh = q_ref[:, h * HEAD_WIDTH:(h + 1) * HEAD_WIDTH] * QK_SCALE
            qr_ref[head_rows(h), :] = jnp.where(lane // DIFF_HEAD_DIM == row, qh, 0.0)
        m_ref[...] = jnp.full(m_ref.shape, NEG_INF, F32)
        l_ref[...] = jnp.zeros(l_ref.shape, F32)
        acc_ref[...] = jnp.zeros(acc_ref.shape, F32)

    s = jnp.concatenate(
        [jnp.concatenate(
            [lax.dot_general(qr_ref[head_rows(h), :], k_refs[p][h], _NT,
                             preferred_element_type=F32) for p in range(pages)], axis=1)
         for h in range(DIFF_HEADS)], axis=0)
    m_prev = m_ref[:, 0:1]
    m_new = jnp.maximum(m_prev, jnp.max(s, axis=1, keepdims=True))
    alpha = jnp.exp(m_prev - m_new)
    e = jnp.exp(s - m_new)
    l_new = alpha * l_ref[:, 0:1] + jnp.sum(e, axis=1, keepdims=True)
    pv = []
    for h in range(DIFF_HEADS):
        acc_h = jnp.dot(e[head_rows(h), 0:PAGE_SIZE], v_refs[0][h], preferred_element_type=F32)
        for p in range(1, pages):
            acc_h = acc_h + jnp.dot(e[head_rows(h), p * PAGE_SIZE:(p + 1) * PAGE_SIZE],
                                    v_refs[p][h], preferred_element_type=F32)
        pv.append(acc_h)
    acc_ref[...] = alpha * acc_ref[...] + jnp.concatenate(pv, axis=0)
    m_ref[...] = jnp.broadcast_to(m_new, m_ref.shape)
    l_ref[...] = jnp.broadcast_to(l_new, l_ref.shape)

    @pl.when(last)
    def _finish():
        lam = _diff_lambda(lam_ref, lam0)
        scale = g_ref[...] * (1.0 - lam0)
        for h in range(DIFF_HEADS):
            cols = slice(h * HEAD_WIDTH, (h + 1) * HEAD_WIDTH)
            m_old = m_ref[head_rows(h), 0:1]
            s_new = jnp.sum(qr_ref[head_rows(h), :] * kn_ref[:, cols], axis=1, keepdims=True)
            m_fin = jnp.maximum(m_old, s_new)
            a_fin = jnp.exp(m_old - m_fin)
            p_new = jnp.exp(s_new - m_fin)
            l_fin = a_fin * l_ref[head_rows(h), 0:1] + p_new
            acc_fin = a_fin * acc_ref[head_rows(h), :] + p_new * vn_ref[:, cols]
            coef = jnp.where(row1 == 0, 1.0, -lam) / l_fin
            coef = jnp.where(row1 < 2, coef, 0.0)
            oh = jnp.sum(acc_fin * coef, axis=0, keepdims=True)
            ms = jnp.mean(oh * oh, axis=1, keepdims=True)
            o_ref[:, cols] = oh * lax.rsqrt(ms + LN_EPS) * scale


def _diff_attn_decode(q, k_new, v_new, cache_k, cache_v, page_table, lam_qk, g, lam0):
    db = q.shape[0]
    n_pages = page_table.shape[1]
    pages = DECODE_PAGES_PER_STEP
    assert n_pages % pages == 0
    q3 = q.reshape(db, 1, SELF_WIDTH)
    kn3 = k_new.reshape(db, 1, SELF_WIDTH)
    vn3 = v_new.reshape(db, 1, SELF_WIDTH)
    pt = page_table.reshape(-1)
    vec = pl.BlockSpec((None, 1, SELF_WIDTH), lambda b, j, pt: (b, 0, 0))

    def page_spec(p):
        return pl.BlockSpec(
            (None, DIFF_HEADS, PAGE_SIZE, HEAD_WIDTH),
            lambda b, j, pt: (pt[b * n_pages + j * pages + p], 0, 0, 0))

    stat = pltpu.VMEM((DIFF_HEADS * SUBLANES, LANES), F32)
    grid_spec = pltpu.PrefetchScalarGridSpec(
        num_scalar_prefetch=1,
        grid=(db, n_pages // pages),
        in_specs=[vec, vec, vec,
                  pl.BlockSpec(lam_qk.shape, lambda b, j, pt: (0, 0)),
                  pl.BlockSpec(g.shape, lambda b, j, pt: (0, 0))]
                 + [page_spec(p) for p in range(pages)]
                 + [page_spec(p) for p in range(pages)],
        out_specs=vec,
        scratch_shapes=[pltpu.VMEM((DIFF_HEADS * SUBLANES, HEAD_WIDTH), F32), stat, stat,
                        pltpu.VMEM((DIFF_HEADS * SUBLANES, HEAD_WIDTH), F32)],
    )
    out = pl.pallas_call(
        functools.partial(_diff_attn_decode_body, pages=pages, lam0=lam0),
        grid_spec=grid_spec,
        out_shape=jax.ShapeDtypeStruct((db, 1, SELF_WIDTH), F32),
        compiler_params=_params("arbitrary", "arbitrary"),
        name="diff_attn_decode",
    )(pt, q3, kn3, vn3, lam_qk, g, *([cache_k] * pages), *([cache_v] * pages))
    return out.reshape(db, SELF_WIDTH)


def _mem_decode_body(qm_ref, mkt_ref, mvt_ref, o_ref):
    lane = lax.broadcasted_iota(jnp.int32, (SUBLANES, MEM_WIDTH), 1)
    row = lax.broadcasted_iota(jnp.int32, (SUBLANES, MEM_WIDTH), 0)
    own = lane // MEM_HEAD_DIM == row
    for r in range(qm_ref.shape[0]):
        qr = jnp.where(own, qm_ref[r], 0.0)
        s = jnp.dot(qr, mkt_ref[r], preferred_element_type=F32) * MEM_SCALE
        m = jnp.max(s, axis=1, keepdims=True)
        e = jnp.exp(s - m)
        p = e * (1.0 / jnp.sum(e, axis=1, keepdims=True))
        o = lax.dot_general(p, mvt_ref[r], _NT, preferred_element_type=F32)
        o_ref[r] = jnp.sum(jnp.where(own, o, 0.0), axis=0, keepdims=True)


def _mem_decode(qm, mem_kt, mem_vt, layer):
    db = qm.shape[0]
    rows = MEM_DECODE_ROWS_PER_STEP
    vec = pl.BlockSpec((rows, 1, MEM_WIDTH), lambda b: (b, 0, 0))
    mem = pl.BlockSpec((None, rows, MEM_WIDTH, MEM_LEN), lambda b: (layer, b, 0, 0))
    out = pl.pallas_call(
        _mem_decode_body,
        grid=(db // rows,),
        in_specs=[vec, mem, mem],
        out_specs=vec,
        out_shape=jax.ShapeDtypeStruct((db, 1, MEM_WIDTH), F32),
        compiler_params=_params("arbitrary"),
        name="mem_decode",
    )(qm.reshape(db, 1, MEM_WIDTH), mem_kt, mem_vt)
    return out.reshape(db, MEM_WIDTH)


def _sgu_body(x_ref, w_ref, lng_ref, lnb_ref, *refs, prompt):
    if prompt:
        ws_ref, sb_ref, mk_ref, mvm_ref, a_ref, m_ref = refs
    else:
        sc_ref, sh_ref, a_ref, m_ref, sv_ref = refs
    xb = x_ref[...].astype(BF16)
    sw = SELF_WIDTH
    u = _gelu_tanh(jnp.dot(xb, w_ref[:, 0:sw], preferred_element_type=F32))
    vv = _gelu_tanh(jnp.dot(xb, w_ref[:, sw:2 * sw], preferred_element_type=F32))
    qm = jnp.dot(xb, w_ref[:, 2 * sw:], preferred_element_type=F32)
    v = _ln(vv, lng_ref[...], lnb_ref[...])
    if prompt:
        vb = v.astype(BF16)
        tm = vb.shape[0]
        r = lax.broadcasted_iota(jnp.int32, (CHUNK, CHUNK), 0)
        c = lax.broadcasted_iota(jnp.int32, (CHUNK, CHUNK), 1)
        for g in range(SGU_GROUPS):
            wm = jnp.where(r >= c, ws_ref[g], 0.0).astype(BF16)
            cols = slice(g * CHUNK, (g + 1) * CHUNK)
            for n in range(tm // CHUNK):
                rows = slice(n * CHUNK, (n + 1) * CHUNK)
                z = jnp.dot(wm, vb[rows, cols], preferred_element_type=F32) + sb_ref[g]
                a_ref[rows, cols] = (u[rows, cols] * z).astype(BF16)
        m_ref[...] = _mem_attend_shared(qm, mk_ref, mvm_ref).astype(BF16)
    else:
        sv_ref[...] = v
        a_ref[...] = u * (sc_ref[...] * v + sh_ref[...])
        m_ref[...] = qm


def _sgu_prompt(x, w_bf, ln_g, ln_b, sgu_w, sgu_b_full, mk_bf, mvm_bf, layer, seq):
    m = x.shape[0]
    tm = TOKEN_TILE
    per_b = seq // tm
    row = lambda i: (i, 0)
    const2 = lambda i: (0, 0)
    const3 = lambda i: (0, 0, 0)
    return pl.pallas_call(
        functools.partial(_sgu_body, prompt=True),
        grid=(m // tm,),
        in_specs=[
            pl.BlockSpec((tm, D_MODEL), row),
            pl.BlockSpec(w_bf.shape, const2),
            pl.BlockSpec(ln_g.shape, const2),
            pl.BlockSpec(ln_b.shape, const2),
            pl.BlockSpec(sgu_w.shape, const3),
            pl.BlockSpec(sgu_b_full.shape, const3),
            pl.BlockSpec((None, None, MEM_LEN, MEM_WIDTH), lambda i: (layer, i // per_b, 0, 0)),
            pl.BlockSpec((None, None, MEM_HEADS * MEM_LEN, MEM_WIDTH), lambda i: (layer, i // per_b, 0, 0)),
        ],
        out_specs=[pl.BlockSpec((tm, SELF_WIDTH), row), pl.BlockSpec((tm, MEM_WIDTH), row)],
        out_shape=[jax.ShapeDtypeStruct((m, SELF_WIDTH), BF16),
                   jax.ShapeDtypeStruct((m, MEM_WIDTH), BF16)],
        compiler_params=_params("arbitrary"),
        name="sgu_prompt",
    )(x, w_bf, ln_g, ln_b, sgu_w, sgu_b_full, mk_bf, mvm_bf)


def _sgu_sample(x, w_bf, ln_g, ln_b, scale_row, shift_row):
    m = x.shape[0]
    c = lambda i: (0, 0)
    return pl.pallas_call(
        functools.partial(_sgu_body, prompt=False),
        grid=(1,),
        in_specs=[
            pl.BlockSpec((m, D_MODEL), c),
            pl.BlockSpec(w_bf.shape, c),
            pl.BlockSpec(ln_g.shape, c),
            pl.BlockSpec(ln_b.shape, c),
            pl.BlockSpec(scale_row.shape, c),
            pl.BlockSpec(shift_row.shape, c),
        ],
        out_specs=[pl.BlockSpec((m, SELF_WIDTH), c), pl.BlockSpec((m, MEM_WIDTH), c),
                   pl.BlockSpec((m, SELF_WIDTH), c)],
        out_shape=[jax.ShapeDtypeStruct((m, SELF_WIDTH), F32),
                   jax.ShapeDtypeStruct((m, MEM_WIDTH), F32),
                   jax.ShapeDtypeStruct((m, SELF_WIDTH), F32)],
        compiler_params=_params("arbitrary"),
        name="sgu_sample",
    )(x, w_bf, ln_g, ln_b, scale_row, shift_row)


def _out_ffn_body(x_ref, a_ref, m_ref, wo_ref, g1_ref, b1_ref, wu_ref, wd_ref, g2_ref, b2_ref,
                  o_ref):
    x = x_ref[...]
    d = jnp.dot(a_ref[...].astype(BF16), wo_ref[0:SELF_WIDTH, :], preferred_element_type=F32)
    d = d + jnp.dot(m_ref[...].astype(BF16), wo_ref[SELF_WIDTH:, :], preferred_element_type=F32)
    y = _ln(ALPHA * x + d, g1_ref[...], b1_ref[...])
    yb = y.astype(BF16)
    acc = jnp.zeros_like(y)
    for c in range(D_FF // FF_CHUNK):
        cols = slice(c * FF_CHUNK, (c + 1) * FF_CHUNK)
        h = jnp.maximum(jnp.dot(yb, wu_ref[:, cols], preferred_element_type=F32), 0.0)
        acc = acc + jnp.dot((h * h).astype(BF16), wd_ref[cols, :], preferred_element_type=F32)
    o_ref[...] = _ln(ALPHA * y + acc, g2_ref[...], b2_ref[...])


def _out_ffn(x, a, ma, wo_bf, g1, b1, wu_bf, wd_bf, g2, b2, layer, tm, name):
    m = x.shape[0]
    row = lambda i: (i, 0)
    resident = lambda arr: pl.BlockSpec((None,) + arr.shape[1:], lambda i: (layer, 0, 0),
                                        pipeline_mode=pl.Buffered(1))
    return pl.pallas_call(
        _out_ffn_body,
        grid=(m // tm,),
        in_specs=[
            pl.BlockSpec((tm, D_MODEL), row),
            pl.BlockSpec((tm, SELF_WIDTH), row),
            pl.BlockSpec((tm, MEM_WIDTH), row),
            resident(wo_bf), resident(g1), resident(b1),
            resident(wu_bf), resident(wd_bf), resident(g2), resident(b2),
        ],
        out_specs=pl.BlockSpec((tm, D_MODEL), row),
        out_shape=jax.ShapeDtypeStruct((m, D_MODEL), F32),
        compiler_params=_params("arbitrary"),
        name=name,
    )(x, a, ma, wo_bf, g1, b1, wu_bf, wd_bf, g2, b2)


def kernel(x_prompt, x_sample, cache_k, cache_v, cache_mem_k, cache_mem_v, page_table, mem_prompt, w_in_attn, lambda_qk, subln_g, w_in_sgu, sgu_ln_g, sgu_ln_b, sgu_w, sgu_b, w_mem_kv, w_out, ln1_g, ln1_b, w_up, w_down, ln2_g, ln2_b):
    nb, seq, _ = x_prompt.shape
    db = x_sample.shape[0]
    row = lambda a: a.reshape(1, -1)

    xp = x_prompt.reshape(nb * seq, D_MODEL)
    xs = x_sample.reshape(db, D_MODEL)
    w_in_attn_bf = w_in_attn.astype(BF16)
    w_in_sgu_bf = w_in_sgu.astype(BF16)
    w_out_bf = w_out.astype(BF16)
    w_up_bf = w_up.astype(BF16)
    w_down_bf = w_down.astype(BF16)
    cache_k_pg = jnp.transpose(cache_k[0], (0, 2, 1, 3))
    cache_v_pg = jnp.transpose(cache_v[0], (0, 2, 1, 3))
    mem_kt = jnp.transpose(cache_mem_k, (0, 1, 3, 4, 2)).reshape(DEPTH, db, MEM_WIDTH, MEM_LEN)
    mem_vt = jnp.transpose(cache_mem_v, (0, 1, 3, 4, 2)).reshape(DEPTH, db, MEM_WIDTH, MEM_LEN)

    mkt_p, mvt_p, mk_bf, mvm_bf = _mem_kv(mem_prompt, w_mem_kv.astype(BF16))

    ln_rows = lambda a: a.reshape(DEPTH, 1, D_MODEL)
    ln1_g3, ln1_b3, ln2_g3, ln2_b3 = ln_rows(ln1_g), ln_rows(ln1_b), ln_rows(ln2_g), ln_rows(ln2_b)

    def out_ffn(x, a, ma, l, tm, name):
        return _out_ffn(x, a, ma, w_out_bf, ln1_g3, ln1_b3, w_up_bf, w_down_bf, ln2_g3, ln2_b3,
                        l, tm, name)

    lam0 = _lambda_init(0)
    g0 = row(subln_g[0])
    q_bf, k_pg, v_pg, k_bf, vt_bf, ma_p = _attn_proj_prompt(xp, w_in_attn_bf[0], mk_bf, mvm_bf, 0, seq)
    o_p = _diff_attn_prompt(q_bf, k_bf, vt_bf, lambda_qk[0], g0, lam0)
    xp = out_ffn(xp, o_p.reshape(nb * seq, SELF_WIDTH), ma_p, 0, TOKEN_TILE, "out_ffn_prompt0")

    q_s, k_s, v_s, qm_s = _attn_proj_sample(xs, w_in_attn_bf[0])
    o_s = _diff_attn_decode(q_s, k_s, v_s, cache_k_pg, cache_v_pg, page_table, lambda_qk[0], g0, lam0)
    ma_s = _mem_decode(qm_s, mem_kt, mem_vt, 0)
    xs = out_ffn(xs, o_s, ma_s, 0, db, "out_ffn_sample0")

    sgu_b_full = jnp.broadcast_to(sgu_b[0][:, :, None], (SGU_GROUPS, CHUNK, CHUNK))
    a_p, ma_p = _sgu_prompt(xp, w_in_sgu_bf[0], row(sgu_ln_g[0]), row(sgu_ln_b[0]), sgu_w[0],
                            sgu_b_full, mk_bf, mvm_bf, 1, seq)
    xp = out_ffn(xp, a_p, ma_p, 1, TOKEN_TILE, "out_ffn_prompt1")

    scale_row = row(jnp.repeat(sgu_w[0][:, 0, 0], CHUNK))
    shift_row = row(jnp.repeat(sgu_b[0][:, 0], CHUNK))
    a_s, qm_s, sv_s = _sgu_sample(xs, w_in_sgu_bf[0], row(sgu_ln_g[0]), row(sgu_ln_b[0]),
                                  scale_row, shift_row)
    ma_s = _mem_decode(qm_s, mem_kt, mem_vt, 1)
    xs = out_ffn(xs, a_s, ma_s, 1, db, "out_ffn_sample1")

    n_pg = seq // PAGE_SIZE
    paged = lambda a: jnp.transpose(
        a.reshape(1, nb, n_pg, DIFF_HEADS, PAGE_SIZE, HEAD_WIDTH), (0, 1, 2, 4, 3, 5))
    mem_out = lambda a: jnp.transpose(
        a.reshape(DEPTH, nb, MEM_HEADS, MEM_HEAD_DIM, MEM_LEN), (0, 1, 4, 2, 3))
    dec_shape = (1, db, 1, DIFF_HEADS, HEAD_WIDTH)
    return (xp.reshape(nb, seq, D_MODEL), xs.reshape(db, 1, D_MODEL),
            paged(k_pg), paged(v_pg), mem_out(mkt_p), mem_out(mvt_p),
            k_s.reshape(dec_shape), v_s.reshape(dec_shape), sv_s.reshape(dec_shape))
```

```python
import functools
import math

import jax
import jax.numpy as jnp
from jax import lax
from jax.experimental import pallas as pl
from jax.experimental.pallas import tpu as pltpu

F32 = jnp.float32
BF16 = jnp.bfloat16

D_MODEL = 1024
DEPTH = 2
PAGE_SIZE = 128
MEM_LEN = 256
MEM_HEADS = 4
MEM_HEAD_DIM = 64
MEM_WIDTH = MEM_HEADS * MEM_HEAD_DIM
SELF_WIDTH = D_MODEL - MEM_WIDTH
ATTN_IN = 3 * SELF_WIDTH + MEM_WIDTH
DIFF_HEAD_DIM = 64
DIFF_HEADS = SELF_WIDTH // (2 * DIFF_HEAD_DIM)
HEAD_WIDTH = 2 * DIFF_HEAD_DIM
CHUNK = 128
SGU_GROUPS = 6
D_FF = 4 * D_MODEL
ALPHA = (2 * DEPTH) ** 0.25
LN_EPS = 1e-5
NEG_INF = -1e30
QK_SCALE = DIFF_HEAD_DIM ** -0.5
QK_SCALE_LOG2 = QK_SCALE * math.log2(math.e)
MEM_SCALE = MEM_HEAD_DIM ** -0.5

LANES = 128
SUBLANES = 8
BF16_SUBLANES = 16
VT_ROWS = HEAD_WIDTH + BF16_SUBLANES
VMEM_LIMIT_BYTES = 56 * 1024 * 1024

TOKEN_TILE = 512
FFN_TOKEN_TILE = 1024
ROW_SUB_TILE = 256
ATTN_BLOCK = 1024
FF_CHUNK = 1024
DECODE_PAGES_PER_STEP = 16
MEM_DECODE_ROWS_PER_STEP = 8

_NT = (((1,), (1,)), ((), ()))


def _lambda_init(layer):
    return 0.8 - 0.6 * math.exp(-0.3 * layer)


def _params(*sem):
    return pltpu.CompilerParams(dimension_semantics=sem, vmem_limit_bytes=VMEM_LIMIT_BYTES)


def _ln(x, g, b):
    mu = jnp.mean(x, axis=-1, keepdims=True)
    xc = x - mu
    var = jnp.mean(xc * xc, axis=-1, keepdims=True)
    return xc * lax.rsqrt(var + LN_EPS) * g + b


def _gelu_tanh(x):
    c = math.sqrt(2.0 / math.pi)
    return x * (0.5 * (1.0 + jnp.tanh(c * (x + 0.044715 * (x * x * x)))))


def _diff_lambda(lam_ref, lam0):
    l = lam_ref[...]
    a = jnp.sum(l[0:1, :] * l[1:2, :], axis=1, keepdims=True)
    b = jnp.sum(l[2:3, :] * l[3:4, :], axis=1, keepdims=True)
    return jnp.exp(a) - jnp.exp(b) + lam0


def _mem_kv_body(mem_ref, w_ref, mk_ref, mv_ref, mkb_ref, mvm_ref):
    kv = jnp.dot(mem_ref[...].astype(BF16), w_ref[...], preferred_element_type=F32)
    mk = kv[:, :MEM_WIDTH]
    mv = kv[:, MEM_WIDTH:]
    mk_ref[...] = mk.T
    mv_ref[...] = mv.T
    mkb_ref[...] = mk.astype(BF16)
    lane = lax.broadcasted_iota(jnp.int32, mv.shape, 1)
    for h in range(MEM_HEADS):
        mvm_ref[h * MEM_LEN:(h + 1) * MEM_LEN, :] = jnp.where(
            lane // MEM_HEAD_DIM == h, mv, 0.0).astype(BF16)


def _mem_kv(mem_prompt, w_mem_kv_bf):
    nb = mem_prompt.shape[0]
    per_lb = lambda l, b: (l, b, 0, 0)
    return pl.pallas_call(
        _mem_kv_body,
        grid=(DEPTH, nb),
        in_specs=[
            pl.BlockSpec((None, MEM_LEN, D_MODEL), lambda l, b: (b, 0, 0)),
            pl.BlockSpec((None, D_MODEL, 2 * MEM_WIDTH), lambda l, b: (l, 0, 0)),
        ],
        out_specs=[
            pl.BlockSpec((None, None, MEM_WIDTH, MEM_LEN), per_lb),
            pl.BlockSpec((None, None, MEM_WIDTH, MEM_LEN), per_lb),
            pl.BlockSpec((None, None, MEM_LEN, MEM_WIDTH), per_lb),
            pl.BlockSpec((None, None, MEM_HEADS * MEM_LEN, MEM_WIDTH), per_lb),
        ],
        out_shape=[
            jax.ShapeDtypeStruct((DEPTH, nb, MEM_WIDTH, MEM_LEN), F32),
            jax.ShapeDtypeStruct((DEPTH, nb, MEM_WIDTH, MEM_LEN), F32),
            jax.ShapeDtypeStruct((DEPTH, nb, MEM_LEN, MEM_WIDTH), BF16),
            jax.ShapeDtypeStruct((DEPTH, nb, MEM_HEADS * MEM_LEN, MEM_WIDTH), BF16),
        ],
        compiler_params=_params("arbitrary", "arbitrary"),
        name="mem_kv",
    )(mem_prompt, w_mem_kv_bf)


def _mem_attend_shared(qm, mk_ref, mvm_ref):
    lane = lax.broadcasted_iota(jnp.int32, qm.shape, 1)
    mk = mk_ref[...]
    ps = []
    for h in range(MEM_HEADS):
        qh = jnp.where(lane // MEM_HEAD_DIM == h, qm, 0.0).astype(BF16)
        s = lax.dot_general(qh, mk, _NT, preferred_element_type=F32) * MEM_SCALE
        m = jnp.max(s, axis=1, keepdims=True)
        e = jnp.exp(s - m)
        l = jnp.sum(e, axis=1, keepdims=True)
        ps.append((e * (1.0 / l)).astype(BF16))
    p = jnp.concatenate(ps, axis=1)
    return jnp.dot(p, mvm_ref[...], preferred_element_type=F32)


def _attn_proj_body(x_ref, w_ref, *refs, with_mem):
    if with_mem:
        mk_ref, mvm_ref, q_ref, kf_ref, vf_ref, kb_ref, vt_ref, m_ref = refs
    else:
        q_ref, kf_ref, vf_ref, m_ref = refs
    sw = SELF_WIDTH

    def project(rows):
        xb = x_ref[rows, :].astype(BF16)
        return tuple(jnp.dot(xb, w_ref[:, lo:hi], preferred_element_type=F32)
                     for lo, hi in ((0, sw), (sw, 2 * sw), (2 * sw, 3 * sw), (3 * sw, ATTN_IN)))

    if not with_mem:
        q, k, v, qm = project(slice(None))
        kf_ref[...] = k
        vf_ref[...] = v
        q_ref[...] = q
        m_ref[...] = qm
        return

    sub = ROW_SUB_TILE
    tiles = [slice(i * sub, (i + 1) * sub) for i in range(x_ref.shape[0] // sub)]
    nxt = project(tiles[0])
    for i, rows in enumerate(tiles):
        q, k, v, qm = nxt
        if i + 1 < len(tiles):
            nxt = project(tiles[i + 1])
        for h in range(DIFF_HEADS):
            cols = slice(h * HEAD_WIDTH, (h + 1) * HEAD_WIDTH)
            for p in range(sub // PAGE_SIZE):
                rr = slice(p * PAGE_SIZE, (p + 1) * PAGE_SIZE)
                page = rows.start // PAGE_SIZE + p
                kf_ref[page, h] = k[rr, cols]
                vf_ref[page, h] = v[rr, cols]
            q_ref[h, rows, :] = (q[:, cols] * QK_SCALE_LOG2).astype(BF16)
            kb_ref[h, rows, :] = k[:, cols].astype(BF16)
            vt_ref[h, 0:HEAD_WIDTH, rows] = v[:, cols].T.astype(BF16)
            vt_ref[h, HEAD_WIDTH:, rows] = jnp.ones((BF16_SUBLANES, sub), BF16)
        m_ref[rows, :] = _mem_attend_shared(qm, mk_ref, mvm_ref).astype(BF16)


def _attn_proj_prompt(x, w_bf, mk_bf, mvm_bf, layer, seq):
    m = x.shape[0]
    tm = TOKEN_TILE
    per_b = seq // tm
    pages = tm // PAGE_SIZE
    row = lambda i: (i, 0)
    nb = m // seq
    per_blk = ATTN_BLOCK // tm
    paged = pl.BlockSpec((pages, DIFF_HEADS, PAGE_SIZE, HEAD_WIDTH), lambda i: (i, 0, 0, 0))
    paged_shape = jax.ShapeDtypeStruct((m // PAGE_SIZE, DIFF_HEADS, PAGE_SIZE, HEAD_WIDTH), F32)
    heads = pl.BlockSpec((None, DIFF_HEADS, tm, HEAD_WIDTH), lambda i: (i // per_b, 0, i % per_b, 0))
    heads_shape = jax.ShapeDtypeStruct((nb, DIFF_HEADS, seq, HEAD_WIDTH), BF16)
    return pl.pallas_call(
        functools.partial(_attn_proj_body, with_mem=True),
        grid=(m // tm,),
        in_specs=[
            pl.BlockSpec((tm, D_MODEL), row),
            pl.BlockSpec(w_bf.shape, lambda i: (0, 0)),
            pl.BlockSpec((None, None, MEM_LEN, MEM_WIDTH), lambda i: (layer, i // per_b, 0, 0)),
            pl.BlockSpec((None, None, MEM_HEADS * MEM_LEN, MEM_WIDTH), lambda i: (layer, i // per_b, 0, 0)),
        ],
        out_specs=[
            heads,
            paged,
            paged,
            heads,
            pl.BlockSpec((None, DIFF_HEADS, None, VT_ROWS, tm),
                         lambda i: (i // per_b, 0, (i % per_b) // per_blk, 0, (i % per_b) % per_blk)),
            pl.BlockSpec((tm, MEM_WIDTH), row),
        ],
        out_shape=[
            heads_shape,
            paged_shape,
            paged_shape,
            heads_shape,
            jax.ShapeDtypeStruct((nb, DIFF_HEADS, seq // ATTN_BLOCK, VT_ROWS, ATTN_BLOCK), BF16),
            jax.ShapeDtypeStruct((m, MEM_WIDTH), BF16),
        ],
        compiler_params=_params("arbitrary"),
        name="attn_proj_prompt",
    )(x, w_bf, mk_bf, mvm_bf)


def _attn_proj_sample(x, w_bf):
    m = x.shape[0]
    row = lambda i: (0, 0)
    return pl.pallas_call(
        functools.partial(_attn_proj_body, with_mem=False),
        grid=(1,),
        in_specs=[pl.BlockSpec((m, D_MODEL), row), pl.BlockSpec(w_bf.shape, row)],
        out_specs=[
            pl.BlockSpec((m, SELF_WIDTH), row),
            pl.BlockSpec((m, SELF_WIDTH), row),
            pl.BlockSpec((m, SELF_WIDTH), row),
            pl.BlockSpec((m, MEM_WIDTH), row),
        ],
        out_shape=[
            jax.ShapeDtypeStruct((m, SELF_WIDTH), F32),
            jax.ShapeDtypeStruct((m, SELF_WIDTH), F32),
            jax.ShapeDtypeStruct((m, SELF_WIDTH), F32),
            jax.ShapeDtypeStruct((m, MEM_WIDTH), F32),
        ],
        compiler_params=_params("arbitrary"),
        name="attn_proj_sample",
    )(x, w_bf)


def _diff_attn_body(q_ref, k_ref, vt_ref, lam_ref, g_ref, o_ref,
                    qs_ref, m_ref, l_ref, acc_ref, sbuf_ref, *, blk, lam0):
    qi = pl.program_id(2)
    sq = blk // 2
    chains = [(1, 0), (1, 1), (0, 0), (0, 1)]

    def _init():
        q = q_ref[...]
        lane = lax.broadcasted_iota(jnp.int32, q.shape, 1)
        zero = jnp.zeros_like(q)
        qs_ref[0] = jnp.where(lane < DIFF_HEAD_DIM, q, zero)
        qs_ref[1] = jnp.where(lane >= DIFF_HEAD_DIM, q, zero)
        m_ref[...] = jnp.full(m_ref.shape, NEG_INF, F32)
        l_ref[...] = jnp.zeros(l_ref.shape, F32)
        acc_ref[...] = jnp.zeros(acc_ref.shape, F32)

    def cols(qb):
        return slice(qb * sq, (qb + 1) * sq)

    def scores(j, qb, c, nk=blk):
        k = k_ref[pl.ds(pl.multiple_of(j * blk, blk), nk), :]
        return lax.dot_general(k, qs_ref[c, cols(qb), :], _NT, preferred_element_type=F32)

    def causal(s, qb):
        key = lax.broadcasted_iota(jnp.int32, s.shape, 0)
        qry = qb * sq + lax.broadcasted_iota(jnp.int32, s.shape, 1)
        return jnp.where(key <= qry, s, NEG_INF)

    def chain_max(s, qb, c):
        m_prev = m_ref[c, 0:1, cols(qb)]
        m_new = jnp.maximum(m_prev, jnp.max(s, axis=0, keepdims=True))
        return m_new, jnp.exp2(m_prev - m_new)

    def reduce_chain(s, stats, vt, qb, c):
        m_new, alpha = stats
        p = jnp.exp2(s - m_new).astype(BF16)
        pv = jnp.dot(vt, p, preferred_element_type=F32)
        l_new = alpha * l_ref[c, 0:1, cols(qb)] + pv[HEAD_WIDTH:HEAD_WIDTH + 1, :]
        acc_ref[c, :, cols(qb)] = alpha * acc_ref[c, :, cols(qb)] + pv[0:HEAD_WIDTH, :]
        m_ref[c, :, cols(qb)] = jnp.broadcast_to(m_new, (SUBLANES, sq))
        l_ref[c, :, cols(qb)] = jnp.broadcast_to(l_new, (SUBLANES, sq))

    _init()
    sbuf_ref[0] = scores(0, *chains[0])
    sbuf_ref[1] = scores(0, *chains[1])

    def _plain_block(j, carry):
        vt = vt_ref[j]
        s0, s1 = sbuf_ref[0], sbuf_ref[1]
        s2 = scores(j, *chains[2])
        st0 = chain_max(s0, *chains[0])
        st1 = chain_max(s1, *chains[1])
        reduce_chain(s0, st0, vt, *chains[0])
        s3 = scores(j, *chains[3])
        st2 = chain_max(s2, *chains[2])
        reduce_chain(s1, st1, vt, *chains[1])
        sbuf_ref[0] = scores(j + 1, *chains[0])
        st3 = chain_max(s3, *chains[3])
        reduce_chain(s2, st2, vt, *chains[2])
        sbuf_ref[1] = scores(j + 1, *chains[1])
        reduce_chain(s3, st3, vt, *chains[3])
        return carry

    lax.fori_loop(0, qi, _plain_block, 0)

    vt = vt_ref[qi]
    s0 = causal(sbuf_ref[0], chains[0][0])
    s1 = causal(sbuf_ref[1], chains[1][0])
    s2 = causal(scores(qi, *chains[2], nk=sq), chains[2][0])
    st0 = chain_max(s0, *chains[0])
    st1 = chain_max(s1, *chains[1])
    reduce_chain(s0, st0, vt, *chains[0])
    s3 = causal(scores(qi, *chains[3], nk=sq), chains[3][0])
    st2 = chain_max(s2, *chains[2])
    reduce_chain(s1, st1, vt, *chains[1])
    st3 = chain_max(s3, *chains[3])
    reduce_chain(s2, st2, vt[:, 0:sq], *chains[2])
    reduce_chain(s3, st3, vt[:, 0:sq], *chains[3])

    lam = _diff_lambda(lam_ref, lam0)
    inv0 = 1.0 / l_ref[0, 0:1, :]
    inv1 = 1.0 / l_ref[1, 0:1, :]
    ot = acc_ref[0] * inv0 - lam * (acc_ref[1] * inv1)
    ms = jnp.mean(ot * ot, axis=0, keepdims=True)
    yt = ot * lax.rsqrt(ms + LN_EPS)
    o_ref[...] = (yt.T * (g_ref[...] * (1.0 - lam0))).astype(o_ref.dtype)


def _diff_attn_prompt(q_bf, k_bf, vt_bf, lam_qk, g, lam0):
    nb, _, seq, _ = q_bf.shape
    blk = ATTN_BLOCK
    n_blk = seq // blk
    per_head = lambda b, h, i: (b, h, 0, 0)
    return pl.pallas_call(
        functools.partial(_diff_attn_body, blk=blk, lam0=lam0),
        grid=(nb, DIFF_HEADS, n_blk),
        in_specs=[
            pl.BlockSpec((None, None, blk, HEAD_WIDTH), lambda b, h, i: (b, h, i, 0)),
            pl.BlockSpec((None, None, seq, HEAD_WIDTH), per_head),
            pl.BlockSpec((None, None, n_blk, VT_ROWS, blk), lambda b, h, i: (b, h, 0, 0, 0)),
            pl.BlockSpec(lam_qk.shape, lambda b, h, i: (0, 0)),
            pl.BlockSpec(g.shape, lambda b, h, i: (0, 0)),
        ],
        out_specs=pl.BlockSpec((None, blk, HEAD_WIDTH), lambda b, h, i: (b, i, h)),
        out_shape=jax.ShapeDtypeStruct((nb, seq, SELF_WIDTH), BF16),
        scratch_shapes=[
            pltpu.VMEM((2, blk, HEAD_WIDTH), BF16),
            pltpu.VMEM((2, SUBLANES, blk), F32),
            pltpu.VMEM((2, SUBLANES, blk), F32),
            pltpu.VMEM((2, HEAD_WIDTH, blk), F32),
            pltpu.VMEM((2, blk, blk // 2), F32),
        ],
        compiler_params=_params("arbitrary", "arbitrary", "arbitrary"),
        name="diff_attn_prompt",
    )(q_bf, k_bf, vt_bf, lam_qk, g)


def _diff_attn_decode_body(pt_ref, q_ref, kn_ref, vn_ref, lam_ref, g_ref, *refs,
                           pages, lam0):
    k_refs = refs[:pages]
    v_refs = refs[pages:2 * pages]
    o_ref, qr_ref, m_ref, l_ref, acc_ref = refs[2 * pages:]
    j = pl.program_id(1)
    last = j == pl.num_programs(1) - 1
    lane = lax.broadcasted_iota(jnp.int32, (SUBLANES, HEAD_WIDTH), 1)
    row = lax.broadcasted_iota(jnp.int32, (SUBLANES, HEAD_WIDTH), 0)
    row1 = lax.broadcasted_iota(jnp.int32, (SUBLANES, 1), 0)

    def head_rows(h):
        return slice(h * SUBLANES, (h + 1) * SUBLANES)

    @pl.when(j == 0)
    def _init():
        for h in range(DIFF_HEADS):
            qh = q_ref[:, h * HEAD_WIDTH:(h + 1) * HEAD_WIDTH] * QK_SCALE
            qr_ref[head_rows(h), :] = jnp.where(lane // DIFF_HEAD_DIM == row, qh, 0.0)
        m_ref[...] = jnp.full(m_ref.shape, NEG_INF, F32)
        l_ref[...] = jnp.zeros(l_ref.shape, F32)
        acc_ref[...] = jnp.zeros(acc_ref.shape, F32)

    s = jnp.concatenate(
        [jnp.concatenate(
            [lax.dot_general(qr_ref[head_rows(h), :], k_refs[p][h], _NT,
                             preferred_element_type=F32) for p in range(pages)], axis=1)
         for h in range(DIFF_HEADS)], axis=0)
    m_prev = m_ref[:, 0:1]
    m_new = jnp.maximum(m_prev, jnp.max(s, axis=1, keepdims=True))
    alpha = jnp.exp(m_prev - m_new)
    e = jnp.exp(s - m_new)
    l_new = alpha * l_ref[:, 0:1] + jnp.sum(e, axis=1, keepdims=True)
    pv = []
    for h in range(DIFF_HEADS):
        acc_h = jnp.dot(e[head_rows(h), 0:PAGE_SIZE], v_refs[0][h], preferred_element_type=F32)
        for p in range(1, pages):
            acc_h = acc_h + jnp.dot(e[head_rows(h), p * PAGE_SIZE:(p + 1) * PAGE_SIZE],
                                    v_refs[p][h], preferred_element_type=F32)
        pv.append(acc_h)
    acc_ref[...] = alpha * acc_ref[...] + jnp.concatenate(pv, axis=0)
    m_ref[...] = jnp.broadcast_to(m_new, m_ref.shape)
    l_ref[...] = jnp.broadcast_to(l_new, l_ref.shape)

    @pl.when(last)
    def _finish():
        lam = _diff_lambda(lam_ref, lam0)
        scale = g_ref[...] * (1.0 - lam0)
        for h in range(DIFF_HEADS):
            cols = slice(h * HEAD_WIDTH, (h + 1) * HEAD_WIDTH)
            m_old = m_ref[head_rows(h), 0:1]
            s_new = jnp.sum(qr_ref[head_rows(h), :] * kn_ref[:, cols], axis=1, keepdims=True)
            m_fin = jnp.maximum(m_old, s_new)
            a_fin = jnp.exp(m_old - m_fin)
            p_new = jnp.exp(s_new - m_fin)
            l_fin = a_fin * l_ref[head_rows(h), 0:1] + p_new
            acc_fin = a_fin * acc_ref[head_rows(h), :] + p_new * vn_ref[:, cols]
            coef = jnp.where(row1 == 0, 1.0, -lam) / l_fin
            coef = jnp.where(row1 < 2, coef, 0.0)
            oh = jnp.sum(acc_fin * coef, axis=0, keepdims=True)
            ms = jnp.mean(oh * oh, axis=1, keepdims=True)
            o_ref[:, cols] = oh * lax.rsqrt(ms + LN_EPS) * scale


def _diff_attn_decode(q, k_new, v_new, cache_k, cache_v, page_table, lam_qk, g, lam0):
    db = q.shape[0]
    n_pages = page_table.shape[1]
    pages = DECODE_PAGES_PER_STEP
    assert n_pages % pages == 0
    q3 = q.reshape(db, 1, SELF_WIDTH)
    kn3 = k_new.reshape(db, 1, SELF_WIDTH)
    vn3 = v_new.reshape(db, 1, SELF_WIDTH)
    pt = page_table.reshape(-1)
    vec = pl.BlockSpec((None, 1, SELF_WIDTH), lambda b, j, pt: (b, 0, 0))

    def page_spec(p):
        return pl.BlockSpec(
            (None, DIFF_HEADS, PAGE_SIZE, HEAD_WIDTH),
            lambda b, j, pt: (pt[b * n_pages + j * pages + p], 0, 0, 0))

    stat = pltpu.VMEM((DIFF_HEADS * SUBLANES, LANES), F32)
    grid_spec = pltpu.PrefetchScalarGridSpec(
        num_scalar_prefetch=1,
        grid=(db, n_pages // pages),
        in_specs=[vec, vec, vec,
                  pl.BlockSpec(lam_qk.shape, lambda b, j, pt: (0, 0)),
                  pl.BlockSpec(g.shape, lambda b, j, pt: (0, 0))]
                 + [page_spec(p) for p in range(pages)]
                 + [page_spec(p) for p in range(pages)],
        out_specs=vec,
        scratch_shapes=[pltpu.VMEM((DIFF_HEADS * SUBLANES, HEAD_WIDTH), F32), stat, stat,
                        pltpu.VMEM((DIFF_HEADS * SUBLANES, HEAD_WIDTH), F32)],
    )
    out = pl.pallas_call(
        functools.partial(_diff_attn_decode_body, pages=pages, lam0=lam0),
        grid_spec=grid_spec,
        out_shape=jax.ShapeDtypeStruct((db, 1, SELF_WIDTH), F32),
        compiler_params=_params("arbitrary", "arbitrary"),
        name="diff_attn_decode",
    )(pt, q3, kn3, vn3, lam_qk, g, *([cache_k] * pages), *([cache_v] * pages))
    return out.reshape(db, SELF_WIDTH)


def _mem_decode_body(qm_ref, mkt_ref, mvt_ref, o_ref):
    lane = lax.broadcasted_iota(jnp.int32, (SUBLANES, MEM_WIDTH), 1)
    row = lax.broadcasted_iota(jnp.int32, (SUBLANES, MEM_WIDTH), 0)
    own = lane // MEM_HEAD_DIM == row
    for r in range(qm_ref.shape[0]):
        qr = jnp.where(own, qm_ref[r], 0.0)
        s = jnp.dot(qr, mkt_ref[r], preferred_element_type=F32) * MEM_SCALE
        m = jnp.max(s, axis=1, keepdims=True)
        e = jnp.exp(s - m)
        p = e * (1.0 / jnp.sum(e, axis=1, keepdims=True))
        o = lax.dot_general(p, mvt_ref[r], _NT, preferred_element_type=F32)
        o_ref[r] = jnp.sum(jnp.where(own, o, 0.0), axis=0, keepdims=True)


def _mem_decode(qm, mem_kt, mem_vt, layer):
    db = qm.shape[0]
    rows = MEM_DECODE_ROWS_PER_STEP
    vec = pl.BlockSpec((rows, 1, MEM_WIDTH), lambda b: (b, 0, 0))
    mem = pl.BlockSpec((None, rows, MEM_WIDTH, MEM_LEN), lambda b: (layer, b, 0, 0))
    out = pl.pallas_call(
        _mem_decode_body,
        grid=(db // rows,),
        in_specs=[vec, mem, mem],
        out_specs=vec,
        out_shape=jax.ShapeDtypeStruct((db, 1, MEM_WIDTH), F32),
        compiler_params=_params("arbitrary"),
        name="mem_decode",
    )(qm.reshape(db, 1, MEM_WIDTH), mem_kt, mem_vt)
    return out.reshape(db, MEM_WIDTH)


def _sgu_body(x_ref, w_ref, lng_ref, lnb_ref, *refs, prompt):
    if prompt:
        ws_ref, sb_ref, mk_ref, mvm_ref, a_ref, m_ref = refs
    else:
        sc_ref, sh_ref, a_ref, m_ref, sv_ref = refs
    sw = SELF_WIDTH

    def project(rows):
        xb = x_ref[rows, :].astype(BF16)
        return (jnp.dot(xb, w_ref[:, 0:sw], preferred_element_type=F32),
                jnp.dot(xb, w_ref[:, sw:2 * sw], preferred_element_type=F32),
                jnp.dot(xb, w_ref[:, 2 * sw:], preferred_element_type=F32))

    if not prompt:
        up, vp, qm = project(slice(None))
        u = _gelu_tanh(up)
        v = _ln(_gelu_tanh(vp), lng_ref[...], lnb_ref[...])
        sv_ref[...] = v
        a_ref[...] = u * (sc_ref[...] * v + sh_ref[...])
        m_ref[...] = qm
        return

    r = lax.broadcasted_iota(jnp.int32, (CHUNK, CHUNK), 0)
    c = lax.broadcasted_iota(jnp.int32, (CHUNK, CHUNK), 1)
    wms = [jnp.where(r >= c, ws_ref[g], 0.0).astype(BF16) for g in range(SGU_GROUPS)]
    sub = ROW_SUB_TILE
    tiles = [slice(i * sub, (i + 1) * sub) for i in range(x_ref.shape[0] // sub)]
    nxt = project(tiles[0])
    for i, rows in enumerate(tiles):
        up, vp, qm = nxt
        if i + 1 < len(tiles):
            nxt = project(tiles[i + 1])
        u = _gelu_tanh(up)
        vb = _ln(_gelu_tanh(vp), lng_ref[...], lnb_ref[...]).astype(BF16)
        for g in range(SGU_GROUPS):
            cols = slice(g * CHUNK, (g + 1) * CHUNK)
            for n in range(sub // CHUNK):
                rr = slice(n * CHUNK, (n + 1) * CHUNK)
                z = jnp.dot(wms[g], vb[rr, cols], preferred_element_type=F32) + sb_ref[g]
                a_ref[rows.start + n * CHUNK:rows.start + (n + 1) * CHUNK, cols] = (
                    u[rr, cols] * z).astype(BF16)
        m_ref[rows, :] = _mem_attend_shared(qm, mk_ref, mvm_ref).astype(BF16)


def _sgu_prompt(x, w_bf, ln_g, ln_b, sgu_w, sgu_b_full, mk_bf, mvm_bf, layer, seq):
    m = x.shape[0]
    tm = TOKEN_TILE
    per_b = seq // tm
    row = lambda i: (i, 0)
    const2 = lambda i: (0, 0)
    const3 = lambda i: (0, 0, 0)
    return pl.pallas_call(
        functools.partial(_sgu_body, prompt=True),
        grid=(m // tm,),
        in_specs=[
            pl.BlockSpec((tm, D_MODEL), row),
            pl.BlockSpec(w_bf.shape, const2),
            pl.BlockSpec(ln_g.shape, const2),
            pl.BlockSpec(ln_b.shape, const2),
            pl.BlockSpec(sgu_w.shape, const3),
            pl.BlockSpec(sgu_b_full.shape, const3),
            pl.BlockSpec((None, None, MEM_LEN, MEM_WIDTH), lambda i: (layer, i // per_b, 0, 0)),
            pl.BlockSpec((None, None, MEM_HEADS * MEM_LEN, MEM_WIDTH), lambda i: (layer, i // per_b, 0, 0)),
        ],
        out_specs=[pl.BlockSpec((tm, SELF_WIDTH), row), pl.BlockSpec((tm, MEM_WIDTH), row)],
        out_shape=[jax.ShapeDtypeStruct((m, SELF_WIDTH), BF16),
                   jax.ShapeDtypeStruct((m, MEM_WIDTH), BF16)],
        compiler_params=_params("arbitrary"),
        name="sgu_prompt",
    )(x, w_bf, ln_g, ln_b, sgu_w, sgu_b_full, mk_bf, mvm_bf)


def _sgu_sample(x, w_bf, ln_g, ln_b, scale_row, shift_row):
    m = x.shape[0]
    c = lambda i: (0, 0)
    return pl.pallas_call(
        functools.partial(_sgu_body, prompt=False),
        grid=(1,),
        in_specs=[
            pl.BlockSpec((m, D_MODEL), c),
            pl.BlockSpec(w_bf.shape, c),
            pl.BlockSpec(ln_g.shape, c),
            pl.BlockSpec(ln_b.shape, c),
            pl.BlockSpec(scale_row.shape, c),
            pl.BlockSpec(shift_row.shape, c),
        ],
        out_specs=[pl.BlockSpec((m, SELF_WIDTH), c), pl.BlockSpec((m, MEM_WIDTH), c),
                   pl.BlockSpec((m, SELF_WIDTH), c)],
        out_shape=[jax.ShapeDtypeStruct((m, SELF_WIDTH), F32),
                   jax.ShapeDtypeStruct((m, MEM_WIDTH), F32),
                   jax.ShapeDtypeStruct((m, SELF_WIDTH), F32)],
        compiler_params=_params("arbitrary"),
        name="sgu_sample",
    )(x, w_bf, ln_g, ln_b, scale_row, shift_row)


def _out_ffn_body(x_ref, a_ref, m_ref, wo_ref, g1_ref, b1_ref, wu_ref, wd_ref, g2_ref, b2_ref,
                  o_ref):
    def mix_proj(rows):
        d = jnp.dot(a_ref[rows, :].astype(BF16), wo_ref[0:SELF_WIDTH, :],
                    preferred_element_type=F32)
        return d + jnp.dot(m_ref[rows, :].astype(BF16), wo_ref[SELF_WIDTH:, :],
                           preferred_element_type=F32)

    tm = x_ref.shape[0]
    sub = min(ROW_SUB_TILE, tm)
    tiles = [slice(i * sub, (i + 1) * sub) for i in range(tm // sub)]
    nxt = mix_proj(tiles[0])
    for i, rows in enumerate(tiles):
        d = nxt
        if i + 1 < len(tiles):
            nxt = mix_proj(tiles[i + 1])
        y = _ln(ALPHA * x_ref[rows, :] + d, g1_ref[...], b1_ref[...])
        yb = y.astype(BF16)
        acc = jnp.zeros_like(y)
        for c in range(D_FF // FF_CHUNK):
            cols = slice(c * FF_CHUNK, (c + 1) * FF_CHUNK)
            h = jnp.maximum(jnp.dot(yb, wu_ref[:, cols], preferred_element_type=F32), 0.0)
            acc = acc + jnp.dot((h * h).astype(BF16), wd_ref[cols, :],
                                preferred_element_type=F32)
        o_ref[rows, :] = _ln(ALPHA * y + acc, g2_ref[...], b2_ref[...])


def _out_ffn(x, a, ma, wo_bf, g1, b1, wu_bf, wd_bf, g2, b2, layer, tm, name):
    m = x.shape[0]
    row = lambda i: (i, 0)
    resident = lambda arr: pl.BlockSpec((None,) + arr.shape[1:], lambda i: (layer, 0, 0),
                                        pipeline_mode=pl.Buffered(1))
    return pl.pallas_call(
        _out_ffn_body,
        grid=(m // tm,),
        in_specs=[
            pl.BlockSpec((tm, D_MODEL), row),
            pl.BlockSpec((tm, SELF_WIDTH), row),
            pl.BlockSpec((tm, MEM_WIDTH), row),
            resident(wo_bf), resident(g1), resident(b1),
            resident(wu_bf), resident(wd_bf), resident(g2), resident(b2),
        ],
        out_specs=pl.BlockSpec((tm, D_MODEL), row),
        out_shape=jax.ShapeDtypeStruct((m, D_MODEL), F32),
        compiler_params=_params("arbitrary"),
        name=name,
    )(x, a, ma, wo_bf, g1, b1, wu_bf, wd_bf, g2, b2)


def kernel(x_prompt, x_sample, cache_k, cache_v, cache_mem_k, cache_mem_v, page_table, mem_prompt, w_in_attn, lambda_qk, subln_g, w_in_sgu, sgu_ln_g, sgu_ln_b, sgu_w, sgu_b, w_mem_kv, w_out, ln1_g, ln1_b, w_up, w_down, ln2_g, ln2_b):
    nb, seq, _ = x_prompt.shape
    db = x_sample.shape[0]
    row = lambda a: a.reshape(1, -1)

    xp = x_prompt.reshape(nb * seq, D_MODEL)
    xs = x_sample.reshape(db, D_MODEL)
    w_in_attn_bf = w_in_attn.astype(BF16)
    w_in_sgu_bf = w_in_sgu.astype(BF16)
    w_out_bf = w_out.astype(BF16)
    w_up_bf = w_up.astype(BF16)
    w_down_bf = w_down.astype(BF16)
    cache_k_pg = jnp.transpose(cache_k[0], (0, 2, 1, 3))
    cache_v_pg = jnp.transpose(cache_v[0], (0, 2, 1, 3))
    mem_kt = jnp.transpose(cache_mem_k, (0, 1, 3, 4, 2)).reshape(DEPTH, db, MEM_WIDTH, MEM_LEN)
    mem_vt = jnp.transpose(cache_mem_v, (0, 1, 3, 4, 2)).reshape(DEPTH, db, MEM_WIDTH, MEM_LEN)

    mkt_p, mvt_p, mk_bf, mvm_bf = _mem_kv(mem_prompt, w_mem_kv.astype(BF16))

    ln_rows = lambda a: a.reshape(DEPTH, 1, D_MODEL)
    ln1_g3, ln1_b3, ln2_g3, ln2_b3 = ln_rows(ln1_g), ln_rows(ln1_b), ln_rows(ln2_g), ln_rows(ln2_b)

    def out_ffn(x, a, ma, l, tm, name):
        return _out_ffn(x, a, ma, w_out_bf, ln1_g3, ln1_b3, w_up_bf, w_down_bf, ln2_g3, ln2_b3,
                        l, tm, name)

    lam0 = _lambda_init(0)
    g0 = row(subln_g[0])
    q_bf, k_pg, v_pg, k_bf, vt_bf, ma_p = _attn_proj_prompt(xp, w_in_attn_bf[0], mk_bf, mvm_bf, 0, seq)
    o_p = _diff_attn_prompt(q_bf, k_bf, vt_bf, lambda_qk[0], g0, lam0)
    xp = out_ffn(xp, o_p.reshape(nb * seq, SELF_WIDTH), ma_p, 0, FFN_TOKEN_TILE, "out_ffn_prompt0")

    q_s, k_s, v_s, qm_s = _attn_proj_sample(xs, w_in_attn_bf[0])
    o_s = _diff_attn_decode(q_s, k_s, v_s, cache_k_pg, cache_v_pg, page_table, lambda_qk[0], g0, lam0)
    ma_s = _mem_decode(qm_s, mem_kt, mem_vt, 0)
    xs = out_ffn(xs, o_s, ma_s, 0, db, "out_ffn_sample0")

    sgu_b_full = jnp.broadcast_to(sgu_b[0][:, :, None], (SGU_GROUPS, CHUNK, CHUNK))
    a_p, ma_p = _sgu_prompt(xp, w_in_sgu_bf[0], row(sgu_ln_g[0]), row(sgu_ln_b[0]), sgu_w[0],
                            sgu_b_full, mk_bf, mvm_bf, 1, seq)
    xp = out_ffn(xp, a_p, ma_p, 1, FFN_TOKEN_TILE, "out_ffn_prompt1")

    scale_row = row(jnp.repeat(sgu_w[0][:, 0, 0], CHUNK))
    shift_row = row(jnp.repeat(sgu_b[0][:, 0], CHUNK))
    a_s, qm_s, sv_s = _sgu_sample(xs, w_in_sgu_bf[0], row(sgu_ln_g[0]), row(sgu_ln_b[0]),
                                  scale_row, shift_row)
    ma_s = _mem_decode(qm_s, mem_kt, mem_vt, 1)
    xs = out_ffn(xs, a_s, ma_s, 1, db, "out_ffn_sample1")

    n_pg = seq // PAGE_SIZE
    paged = lambda a: jnp.transpose(
        a.reshape(1, nb, n_pg, DIFF_HEADS, PAGE_SIZE, HEAD_WIDTH), (0, 1, 2, 4, 3, 5))
    mem_out = lambda a: jnp.transpose(
        a.reshape(DEPTH, nb, MEM_HEADS, MEM_HEAD_DIM, MEM_LEN), (0, 1, 4, 2, 3))
    dec_shape = (1, db, 1, DIFF_HEADS, HEAD_WIDTH)
    return (xp.reshape(nb, seq, D_MODEL), xs.reshape(db, 1, D_MODEL),
            paged(k_pg), paged(v_pg), mem_out(mkt_p), mem_out(mvt_p),
            k_s.reshape(dec_shape), v_s.reshape(dec_shape), sv_s.reshape(dec_shape))
```

```python
import functools
import math

import jax
import jax.numpy as jnp
from jax import lax
from jax.experimental import pallas as pl
from jax.experimental.pallas import tpu as pltpu

F32 = jnp.float32
BF16 = jnp.bfloat16

D_MODEL = 1024
DEPTH = 2
PAGE_SIZE = 128
MEM_LEN = 256
MEM_HEADS = 4
MEM_HEAD_DIM = 64
MEM_WIDTH = MEM_HEADS * MEM_HEAD_DIM
SELF_WIDTH = D_MODEL - MEM_WIDTH
ATTN_IN = 3 * SELF_WIDTH + MEM_WIDTH
DIFF_HEAD_DIM = 64
DIFF_HEADS = SELF_WIDTH // (2 * DIFF_HEAD_DIM)
HEAD_WIDTH = 2 * DIFF_HEAD_DIM
CHUNK = 128
SGU_GROUPS = 6
D_FF = 4 * D_MODEL
ALPHA = (2 * DEPTH) ** 0.25
LN_EPS = 1e-5
NEG_INF = -1e30
QK_SCALE = DIFF_HEAD_DIM ** -0.5
QK_SCALE_LOG2 = QK_SCALE * math.log2(math.e)
MEM_SCALE = MEM_HEAD_DIM ** -0.5

LANES = 128
SUBLANES = 8
BF16_SUBLANES = 16
VT_ROWS = HEAD_WIDTH + BF16_SUBLANES
VMEM_LIMIT_BYTES = 56 * 1024 * 1024

TOKEN_TILE = 1024
FFN_TOKEN_TILE = 1024
ROW_SUB_TILE = 256
ATTN_BLOCK = 1024
FF_CHUNK = 1024
DECODE_PAGES_PER_STEP = 16
MEM_DECODE_ROWS_PER_STEP = 8

_NT = (((1,), (1,)), ((), ()))


def _lambda_init(layer):
    return 0.8 - 0.6 * math.exp(-0.3 * layer)


def _params(*sem):
    return pltpu.CompilerParams(dimension_semantics=sem, vmem_limit_bytes=VMEM_LIMIT_BYTES)


def _ln(x, g, b):
    mu = jnp.mean(x, axis=-1, keepdims=True)
    xc = x - mu
    var = jnp.mean(xc * xc, axis=-1, keepdims=True)
    return xc * lax.rsqrt(var + LN_EPS) * g + b


def _gelu_tanh(x):
    c = math.sqrt(2.0 / math.pi)
    return x * (0.5 * (1.0 + jnp.tanh(c * (x + 0.044715 * (x * x * x)))))


def _diff_lambda(lam_ref, lam0):
    l = lam_ref[...]
    a = jnp.sum(l[0:1, :] * l[1:2, :], axis=1, keepdims=True)
    b = jnp.sum(l[2:3, :] * l[3:4, :], axis=1, keepdims=True)
    return jnp.exp(a) - jnp.exp(b) + lam0


def _mem_kv_body(mem_ref, w_ref, mk_ref, mv_ref, mkb_ref, mvm_ref):
    kv = jnp.dot(mem_ref[...].astype(BF16), w_ref[...], preferred_element_type=F32)
    mk = kv[:, :MEM_WIDTH]
    mv = kv[:, MEM_WIDTH:]
    mk_ref[...] = mk.T
    mv_ref[...] = mv.T
    mkb_ref[...] = mk.astype(BF16)
    lane = lax.broadcasted_iota(jnp.int32, mv.shape, 1)
    for h in range(MEM_HEADS):
        mvm_ref[h * MEM_LEN:(h + 1) * MEM_LEN, :] = jnp.where(
            lane // MEM_HEAD_DIM == h, mv, 0.0).astype(BF16)


def _mem_kv(mem_prompt, w_mem_kv_bf):
    nb = mem_prompt.shape[0]
    per_lb = lambda l, b: (l, b, 0, 0)
    return pl.pallas_call(
        _mem_kv_body,
        grid=(DEPTH, nb),
        in_specs=[
            pl.BlockSpec((None, MEM_LEN, D_MODEL), lambda l, b: (b, 0, 0)),
            pl.BlockSpec((None, D_MODEL, 2 * MEM_WIDTH), lambda l, b: (l, 0, 0)),
        ],
        out_specs=[
            pl.BlockSpec((None, None, MEM_WIDTH, MEM_LEN), per_lb),
            pl.BlockSpec((None, None, MEM_WIDTH, MEM_LEN), per_lb),
            pl.BlockSpec((None, None, MEM_LEN, MEM_WIDTH), per_lb),
            pl.BlockSpec((None, None, MEM_HEADS * MEM_LEN, MEM_WIDTH), per_lb),
        ],
        out_shape=[
            jax.ShapeDtypeStruct((DEPTH, nb, MEM_WIDTH, MEM_LEN), F32),
            jax.ShapeDtypeStruct((DEPTH, nb, MEM_WIDTH, MEM_LEN), F32),
            jax.ShapeDtypeStruct((DEPTH, nb, MEM_LEN, MEM_WIDTH), BF16),
            jax.ShapeDtypeStruct((DEPTH, nb, MEM_HEADS * MEM_LEN, MEM_WIDTH), BF16),
        ],
        compiler_params=_params("arbitrary", "arbitrary"),
        name="mem_kv",
    )(mem_prompt, w_mem_kv_bf)


def _mem_attend_shared(qm, mk_ref, mvm_ref):
    lane = lax.broadcasted_iota(jnp.int32, qm.shape, 1)
    mk = mk_ref[...]
    ps = []
    for h in range(MEM_HEADS):
        qh = jnp.where(lane // MEM_HEAD_DIM == h, qm, 0.0).astype(BF16)
        s = lax.dot_general(qh, mk, _NT, preferred_element_type=F32) * MEM_SCALE
        m = jnp.max(s, axis=1, keepdims=True)
        e = jnp.exp(s - m)
        l = jnp.sum(e, axis=1, keepdims=True)
        ps.append((e * (1.0 / l)).astype(BF16))
    p = jnp.concatenate(ps, axis=1)
    return jnp.dot(p, mvm_ref[...], preferred_element_type=F32)


def _attn_proj_body(x_ref, w_ref, *refs, with_mem):
    if with_mem:
        mk_ref, mvm_ref, q_ref, kf_ref, vf_ref, kb_ref, vt_ref, m_ref = refs
    else:
        q_ref, kf_ref, vf_ref, m_ref = refs
    sw = SELF_WIDTH

    def project(rows):
        xb = x_ref[rows, :].astype(BF16)
        return tuple(jnp.dot(xb, w_ref[:, lo:hi], preferred_element_type=F32)
                     for lo, hi in ((0, sw), (sw, 2 * sw), (2 * sw, 3 * sw), (3 * sw, ATTN_IN)))

    if not with_mem:
        q, k, v, qm = project(slice(None))
        kf_ref[...] = k
        vf_ref[...] = v
        q_ref[...] = q
        m_ref[...] = qm
        return

    sub = ROW_SUB_TILE
    tiles = [slice(i * sub, (i + 1) * sub) for i in range(x_ref.shape[0] // sub)]
    nxt = project(tiles[0])
    for i, rows in enumerate(tiles):
        q, k, v, qm = nxt
        if i + 1 < len(tiles):
            nxt = project(tiles[i + 1])
        for h in range(DIFF_HEADS):
            cols = slice(h * HEAD_WIDTH, (h + 1) * HEAD_WIDTH)
            for p in range(sub // PAGE_SIZE):
                rr = slice(p * PAGE_SIZE, (p + 1) * PAGE_SIZE)
                page = rows.start // PAGE_SIZE + p
                kf_ref[page, h] = k[rr, cols]
                vf_ref[page, h] = v[rr, cols]
            q_ref[h, rows, :] = (q[:, cols] * QK_SCALE_LOG2).astype(BF16)
            kb_ref[h, rows, :] = k[:, cols].astype(BF16)
            vt_ref[h, 0:HEAD_WIDTH, rows] = v[:, cols].T.astype(BF16)
            vt_ref[h, HEAD_WIDTH:, rows] = jnp.ones((BF16_SUBLANES, sub), BF16)
        m_ref[rows, :] = _mem_attend_shared(qm, mk_ref, mvm_ref).astype(BF16)


def _attn_proj_prompt(x, w_bf, mk_bf, mvm_bf, layer, seq):
    m = x.shape[0]
    tm = TOKEN_TILE
    per_b = seq // tm
    pages = tm // PAGE_SIZE
    row = lambda i: (i, 0)
    nb = m // seq
    per_blk = ATTN_BLOCK // tm
    paged = pl.BlockSpec((pages, DIFF_HEADS, PAGE_SIZE, HEAD_WIDTH), lambda i: (i, 0, 0, 0))
    paged_shape = jax.ShapeDtypeStruct((m // PAGE_SIZE, DIFF_HEADS, PAGE_SIZE, HEAD_WIDTH), F32)
    heads = pl.BlockSpec((None, DIFF_HEADS, tm, HEAD_WIDTH), lambda i: (i // per_b, 0, i % per_b, 0))
    heads_shape = jax.ShapeDtypeStruct((nb, DIFF_HEADS, seq, HEAD_WIDTH), BF16)
    return pl.pallas_call(
        functools.partial(_attn_proj_body, with_mem=True),
        grid=(m // tm,),
        in_specs=[
            pl.BlockSpec((tm, D_MODEL), row),
            pl.BlockSpec(w_bf.shape, lambda i: (0, 0)),
            pl.BlockSpec((None, None, MEM_LEN, MEM_WIDTH), lambda i: (layer, i // per_b, 0, 0)),
            pl.BlockSpec((None, None, MEM_HEADS * MEM_LEN, MEM_WIDTH), lambda i: (layer, i // per_b, 0, 0)),
        ],
        out_specs=[
            heads,
            paged,
            paged,
            heads,
            pl.BlockSpec((None, DIFF_HEADS, None, VT_ROWS, tm),
                         lambda i: (i // per_b, 0, (i % per_b) // per_blk, 0, (i % per_b) % per_blk)),
            pl.BlockSpec((tm, MEM_WIDTH), row),
        ],
        out_shape=[
            heads_shape,
            paged_shape,
            paged_shape,
            heads_shape,
            jax.ShapeDtypeStruct((nb, DIFF_HEADS, seq // ATTN_BLOCK, VT_ROWS, ATTN_BLOCK), BF16),
            jax.ShapeDtypeStruct((m, MEM_WIDTH), BF16),
        ],
        compiler_params=_params("arbitrary"),
        name="attn_proj_prompt",
    )(x, w_bf, mk_bf, mvm_bf)


def _attn_proj_sample(x, w_bf):
    m = x.shape[0]
    row = lambda i: (0, 0)
    return pl.pallas_call(
        functools.partial(_attn_proj_body, with_mem=False),
        grid=(1,),
        in_specs=[pl.BlockSpec((m, D_MODEL), row), pl.BlockSpec(w_bf.shape, row)],
        out_specs=[
            pl.BlockSpec((m, SELF_WIDTH), row),
            pl.BlockSpec((m, SELF_WIDTH), row),
            pl.BlockSpec((m, SELF_WIDTH), row),
            pl.BlockSpec((m, MEM_WIDTH), row),
        ],
        out_shape=[
            jax.ShapeDtypeStruct((m, SELF_WIDTH), F32),
            jax.ShapeDtypeStruct((m, SELF_WIDTH), F32),
            jax.ShapeDtypeStruct((m, SELF_WIDTH), F32),
            jax.ShapeDtypeStruct((m, MEM_WIDTH), F32),
        ],
        compiler_params=_params("arbitrary"),
        name="attn_proj_sample",
    )(x, w_bf)


def _diff_attn_body(q_ref, k_ref, vt_ref, lam_ref, g_ref, o_ref,
                    qs_ref, m_ref, l_ref, acc_ref, sbuf_ref, *, blk, lam0):
    qi = pl.program_id(2)
    sq = blk // 2
    chains = [(1, 0), (1, 1), (0, 0), (0, 1)]

    def _init():
        q = q_ref[...]
        lane = lax.broadcasted_iota(jnp.int32, q.shape, 1)
        zero = jnp.zeros_like(q)
        qs_ref[0] = jnp.where(lane < DIFF_HEAD_DIM, q, zero)
        qs_ref[1] = jnp.where(lane >= DIFF_HEAD_DIM, q, zero)
        m_ref[...] = jnp.full(m_ref.shape, NEG_INF, F32)
        l_ref[...] = jnp.zeros(l_ref.shape, F32)
        acc_ref[...] = jnp.zeros(acc_ref.shape, F32)

    def cols(qb):
        return slice(qb * sq, (qb + 1) * sq)

    def scores(j, qb, c, nk=blk):
        k = k_ref[pl.ds(pl.multiple_of(j * blk, blk), nk), :]
        return lax.dot_general(k, qs_ref[c, cols(qb), :], _NT, preferred_element_type=F32)

    def causal(s, qb):
        key = lax.broadcasted_iota(jnp.int32, s.shape, 0)
        qry = qb * sq + lax.broadcasted_iota(jnp.int32, s.shape, 1)
        return jnp.where(key <= qry, s, NEG_INF)

    def chain_max(s, qb, c):
        m_prev = m_ref[c, 0:1, cols(qb)]
        m_new = jnp.maximum(m_prev, jnp.max(s, axis=0, keepdims=True))
        return m_new, jnp.exp2(m_prev - m_new)

    def reduce_chain(s, stats, vt, qb, c):
        m_new, alpha = stats
        p = jnp.exp2(s - m_new).astype(BF16)
        pv = jnp.dot(vt, p, preferred_element_type=F32)
        l_new = alpha * l_ref[c, 0:1, cols(qb)] + pv[HEAD_WIDTH:HEAD_WIDTH + 1, :]
        acc_ref[c, :, cols(qb)] = alpha * acc_ref[c, :, cols(qb)] + pv[0:HEAD_WIDTH, :]
        m_ref[c, :, cols(qb)] = jnp.broadcast_to(m_new, (SUBLANES, sq))
        l_ref[c, :, cols(qb)] = jnp.broadcast_to(l_new, (SUBLANES, sq))

    _init()
    sbuf_ref[0] = scores(0, *chains[0])
    sbuf_ref[1] = scores(0, *chains[1])

    def _plain_block(j):
        vt = vt_ref[j]
        s0, s1 = sbuf_ref[0], sbuf_ref[1]
        s2 = scores(j, *chains[2])
        st0 = chain_max(s0, *chains[0])
        st1 = chain_max(s1, *chains[1])
        reduce_chain(s0, st0, vt, *chains[0])
        s3 = scores(j, *chains[3])
        st2 = chain_max(s2, *chains[2])
        reduce_chain(s1, st1, vt, *chains[1])
        sbuf_ref[0] = scores(j + 1, *chains[0])
        st3 = chain_max(s3, *chains[3])
        reduce_chain(s2, st2, vt, *chains[2])
        sbuf_ref[1] = scores(j + 1, *chains[1])
        reduce_chain(s3, st3, vt, *chains[3])

    def _two_blocks(i, carry):
        _plain_block(2 * i)
        _plain_block(2 * i + 1)
        return carry

    def _last_odd_block(i, carry):
        _plain_block(qi - 1)
        return carry

    lax.fori_loop(0, lax.shift_right_logical(qi, 1), _two_blocks, 0)
    lax.fori_loop(0, qi & 1, _last_odd_block, 0)

    vt = vt_ref[qi]
    s0 = causal(sbuf_ref[0], chains[0][0])
    s1 = causal(sbuf_ref[1], chains[1][0])
    s2 = causal(scores(qi, *chains[2], nk=sq), chains[2][0])
    st0 = chain_max(s0, *chains[0])
    st1 = chain_max(s1, *chains[1])
    reduce_chain(s0, st0, vt, *chains[0])
    s3 = causal(scores(qi, *chains[3], nk=sq), chains[3][0])
    st2 = chain_max(s2, *chains[2])
    reduce_chain(s1, st1, vt, *chains[1])
    st3 = chain_max(s3, *chains[3])
    reduce_chain(s2, st2, vt[:, 0:sq], *chains[2])
    reduce_chain(s3, st3, vt[:, 0:sq], *chains[3])

    lam = _diff_lambda(lam_ref, lam0)
    inv0 = 1.0 / l_ref[0, 0:1, :]
    inv1 = 1.0 / l_ref[1, 0:1, :]
    ot = acc_ref[0] * inv0 - lam * (acc_ref[1] * inv1)
    ms = jnp.mean(ot * ot, axis=0, keepdims=True)
    yt = ot * lax.rsqrt(ms + LN_EPS)
    o_ref[...] = (yt.T * (g_ref[...] * (1.0 - lam0))).astype(o_ref.dtype)


def _diff_attn_prompt(q_bf, k_bf, vt_bf, lam_qk, g, lam0):
    nb, _, seq, _ = q_bf.shape
    blk = ATTN_BLOCK
    n_blk = seq // blk
    per_head = lambda b, h, i: (b, h, 0, 0)
    return pl.pallas_call(
        functools.partial(_diff_attn_body, blk=blk, lam0=lam0),
        grid=(nb, DIFF_HEADS, n_blk),
        in_specs=[
            pl.BlockSpec((None, None, blk, HEAD_WIDTH), lambda b, h, i: (b, h, i, 0)),
            pl.BlockSpec((None, None, seq, HEAD_WIDTH), per_head),
            pl.BlockSpec((None, None, n_blk, VT_ROWS, blk), lambda b, h, i: (b, h, 0, 0, 0)),
            pl.BlockSpec(lam_qk.shape, lambda b, h, i: (0, 0)),
            pl.BlockSpec(g.shape, lambda b, h, i: (0, 0)),
        ],
        out_specs=pl.BlockSpec((None, blk, HEAD_WIDTH), lambda b, h, i: (b, i, h)),
        out_shape=jax.ShapeDtypeStruct((nb, seq, SELF_WIDTH), BF16),
        scratch_shapes=[
            pltpu.VMEM((2, blk, HEAD_WIDTH), BF16),
            pltpu.VMEM((2, SUBLANES, blk), F32),
            pltpu.VMEM((2, SUBLANES, blk), F32),
            pltpu.VMEM((2, HEAD_WIDTH, blk), F32),
            pltpu.VMEM((2, blk, blk // 2), F32),
        ],
        compiler_params=_params("arbitrary", "arbitrary", "arbitrary"),
        name="diff_attn_prompt",
    )(q_bf, k_bf, vt_bf, lam_qk, g)


def _diff_attn_decode_body(pt_ref, q_ref, kn_ref, vn_ref, lam_ref, g_ref, *refs,
                           pages, lam0):
    k_refs = refs[:pages]
    v_refs = refs[pages:2 * pages]
    o_ref, qr_ref, m_ref, l_ref, acc_ref = refs[2 * pages:]
    j = pl.program_id(1)
    last = j == pl.num_programs(1) - 1
    lane = lax.broadcasted_iota(jnp.int32, (SUBLANES, HEAD_WIDTH), 1)
    row = lax.broadcasted_iota(jnp.int32, (SUBLANES, HEAD_WIDTH), 0)
    row1 = lax.broadcasted_iota(jnp.int32, (SUBLANES, 1), 0)

    def head_rows(h):
        return slice(h * SUBLANES, (h + 1) * SUBLANES)

    @pl.when(j == 0)
    def _init():
        for h in range(DIFF_HEADS):
            qh = q_ref[:, h * HEAD_WIDTH:(h + 1) * HEAD_WIDTH] * QK_SCALE
            qr_ref[head_rows(h), :] = jnp.where(lane // DIFF_HEAD_DIM == row, qh, 0.0)
        m_ref[...] = jnp.full(m_ref.shape, NEG_INF, F32)
        l_ref[...] = jnp.zeros(l_ref.shape, F32)
        acc_ref[...] = jnp.zeros(acc_ref.shape, F32)

    s = jnp.concatenate(
        [jnp.concatenate(
            [lax.dot_general(qr_ref[head_rows(h), :], k_refs[p][h], _NT,
                             preferred_element_type=F32) for p in range(pages)], axis=1)
         for h in range(DIFF_HEADS)], axis=0)
    m_prev = m_ref[:, 0:1]
    m_new = jnp.maximum(m_prev, jnp.max(s, axis=1, keepdims=True))
    alpha = jnp.exp(m_prev - m_new)
    e = jnp.exp(s - m_new)
    l_new = alpha * l_ref[:, 0:1] + jnp.sum(e, axis=1, keepdims=True)
    pv = []
    for h in range(DIFF_HEADS):
        acc_h = jnp.dot(e[head_rows(h), 0:PAGE_SIZE], v_refs[0][h], preferred_element_type=F32)
        for p in range(1, pages):
            acc_h = acc_h + jnp.dot(e[head_rows(h), p * PAGE_SIZE:(p + 1) * PAGE_SIZE],
                                    v_refs[p][h], preferred_element_type=F32)
        pv.append(acc_h)
    acc_ref[...] = alpha * acc_ref[...] + jnp.concatenate(pv, axis=0)
    m_ref[...] = jnp.broadcast_to(m_new, m_ref.shape)
    l_ref[...] = jnp.broadcast_to(l_new, l_ref.shape)

    @pl.when(last)
    def _finish():
        lam = _diff_lambda(lam_ref, lam0)
        scale = g_ref[...] * (1.0 - lam0)
        for h in range(DIFF_HEADS):
            cols = slice(h * HEAD_WIDTH, (h + 1) * HEAD_WIDTH)
            m_old = m_ref[head_rows(h), 0:1]
            s_new = jnp.sum(qr_ref[head_rows(h), :] * kn_ref[:, cols], axis=1, keepdims=True)
            m_fin = jnp.maximum(m_old, s_new)
            a_fin = jnp.exp(m_old - m_fin)
            p_new = jnp.exp(s_new - m_fin)
            l_fin = a_fin * l_ref[head_rows(h), 0:1] + p_new
            acc_fin = a_fin * acc_ref[head_rows(h), :] + p_new * vn_ref[:, cols]
            coef = jnp.where(row1 == 0, 1.0, -lam) / l_fin
            coef = jnp.where(row1 < 2, coef, 0.0)
            oh = jnp.sum(acc_fin * coef, axis=0, keepdims=True)
            ms = jnp.mean(oh * oh, axis=1, keepdims=True)
            o_ref[:, cols] = oh * lax.rsqrt(ms + LN_EPS) * scale


def _diff_attn_decode(q, k_new, v_new, cache_k, cache_v, page_table, lam_qk, g, lam0):
    db = q.shape[0]
    n_pages = page_table.shape[1]
    pages = DECODE_PAGES_PER_STEP
    assert n_pages % pages == 0
    q3 = q.reshape(db, 1, SELF_WIDTH)
    kn3 = k_new.reshape(db, 1, SELF_WIDTH)
    vn3 = v_new.reshape(db, 1, SELF_WIDTH)
    pt = page_table.reshape(-1)
    vec = pl.BlockSpec((None, 1, SELF_WIDTH), lambda b, j, pt: (b, 0, 0))

    def page_spec(p):
        return pl.BlockSpec(
            (None, DIFF_HEADS, PAGE_SIZE, HEAD_WIDTH),
            lambda b, j, pt: (pt[b * n_pages + j * pages + p], 0, 0, 0))

    stat = pltpu.VMEM((DIFF_HEADS * SUBLANES, LANES), F32)
    grid_spec = pltpu.PrefetchScalarGridSpec(
        num_scalar_prefetch=1,
        grid=(db, n_pages // pages),
        in_specs=[vec, vec, vec,
                  pl.BlockSpec(lam_qk.shape, lambda b, j, pt: (0, 0)),
                  pl.BlockSpec(g.shape, lambda b, j, pt: (0, 0))]
                 + [page_spec(p) for p in range(pages)]
                 + [page_spec(p) for p in range(pages)],
        out_specs=vec,
        scratch_shapes=[pltpu.VMEM((DIFF_HEADS * SUBLANES, HEAD_WIDTH), F32), stat, stat,
                        pltpu.VMEM((DIFF_HEADS * SUBLANES, HEAD_WIDTH), F32)],
    )
    out = pl.pallas_call(
        functools.partial(_diff_attn_decode_body, pages=pages, lam0=lam0),
        grid_spec=grid_spec,
        out_shape=jax.ShapeDtypeStruct((db, 1, SELF_WIDTH), F32),
        compiler_params=_params("arbitrary", "arbitrary"),
        name="diff_attn_decode",
    )(pt, q3, kn3, vn3, lam_qk, g, *([cache_k] * pages), *([cache_v] * pages))
    return out.reshape(db, SELF_WIDTH)


def _mem_decode_body(qm_ref, mkt_ref, mvt_ref, o_ref):
    lane = lax.broadcasted_iota(jnp.int32, (SUBLANES, MEM_WIDTH), 1)
    row = lax.broadcasted_iota(jnp.int32, (SUBLANES, MEM_WIDTH), 0)
    own = lane // MEM_HEAD_DIM == row
    for r in range(qm_ref.shape[0]):
        qr = jnp.where(own, qm_ref[r], 0.0)
        s = jnp.dot(qr, mkt_ref[r], preferred_element_type=F32) * MEM_SCALE
        m = jnp.max(s, axis=1, keepdims=True)
        e = jnp.exp(s - m)
        p = e * (1.0 / jnp.sum(e, axis=1, keepdims=True))
        o = lax.dot_general(p, mvt_ref[r], _NT, preferred_element_type=F32)
        o_ref[r] = jnp.sum(jnp.where(own, o, 0.0), axis=0, keepdims=True)


def _mem_decode(qm, mem_kt, mem_vt, layer):
    db = qm.shape[0]
    rows = MEM_DECODE_ROWS_PER_STEP
    vec = pl.BlockSpec((rows, 1, MEM_WIDTH), lambda b: (b, 0, 0))
    mem = pl.BlockSpec((None, rows, MEM_WIDTH, MEM_LEN), lambda b: (layer, b, 0, 0))
    out = pl.pallas_call(
        _mem_decode_body,
        grid=(db // rows,),
        in_specs=[vec, mem, mem],
        out_specs=vec,
        out_shape=jax.ShapeDtypeStruct((db, 1, MEM_WIDTH), F32),
        compiler_params=_params("arbitrary"),
        name="mem_decode",
    )(qm.reshape(db, 1, MEM_WIDTH), mem_kt, mem_vt)
    return out.reshape(db, MEM_WIDTH)


def _sgu_body(x_ref, w_ref, lng_ref, lnb_ref, *refs, prompt):
    if prompt:
        ws_ref, sb_ref, mk_ref, mvm_ref, a_ref, m_ref = refs
    else:
        sc_ref, sh_ref, a_ref, m_ref, sv_ref = refs
    sw = SELF_WIDTH

    def project(rows):
        xb = x_ref[rows, :].astype(BF16)
        return (jnp.dot(xb, w_ref[:, 0:sw], preferred_element_type=F32),
                jnp.dot(xb, w_ref[:, sw:2 * sw], preferred_element_type=F32),
                jnp.dot(xb, w_ref[:, 2 * sw:], preferred_element_type=F32))

    if not prompt:
        up, vp, qm = project(slice(None))
        u = _gelu_tanh(up)
        v = _ln(_gelu_tanh(vp), lng_ref[...], lnb_ref[...])
        sv_ref[...] = v
        a_ref[...] = u * (sc_ref[...] * v + sh_ref[...])
        m_ref[...] = qm
        return

    r = lax.broadcasted_iota(jnp.int32, (CHUNK, CHUNK), 0)
    c = lax.broadcasted_iota(jnp.int32, (CHUNK, CHUNK), 1)
    wms = [jnp.where(r >= c, ws_ref[g], 0.0).astype(BF16) for g in range(SGU_GROUPS)]
    sub = ROW_SUB_TILE
    tiles = [slice(i * sub, (i + 1) * sub) for i in range(x_ref.shape[0] // sub)]
    nxt = project(tiles[0])
    for i, rows in enumerate(tiles):
        up, vp, qm = nxt
        if i + 1 < len(tiles):
            nxt = project(tiles[i + 1])
        u = _gelu_tanh(up)
        vb = _ln(_gelu_tanh(vp), lng_ref[...], lnb_ref[...]).astype(BF16)
        for g in range(SGU_GROUPS):
            cols = slice(g * CHUNK, (g + 1) * CHUNK)
            for n in range(sub // CHUNK):
                rr = slice(n * CHUNK, (n + 1) * CHUNK)
                z = jnp.dot(wms[g], vb[rr, cols], preferred_element_type=F32) + sb_ref[g]
                a_ref[rows.start + n * CHUNK:rows.start + (n + 1) * CHUNK, cols] = (
                    u[rr, cols] * z).astype(BF16)
        m_ref[rows, :] = _mem_attend_shared(qm, mk_ref, mvm_ref).astype(BF16)


def _sgu_prompt(x, w_bf, ln_g, ln_b, sgu_w, sgu_b_full, mk_bf, mvm_bf, layer, seq):
    m = x.shape[0]
    tm = TOKEN_TILE
    per_b = seq // tm
    row = lambda i: (i, 0)
    const2 = lambda i: (0, 0)
    const3 = lambda i: (0, 0, 0)
    return pl.pallas_call(
        functools.partial(_sgu_body, prompt=True),
        grid=(m // tm,),
        in_specs=[
            pl.BlockSpec((tm, D_MODEL), row),
            pl.BlockSpec(w_bf.shape, const2),
            pl.BlockSpec(ln_g.shape, const2),
            pl.BlockSpec(ln_b.shape, const2),
            pl.BlockSpec(sgu_w.shape, const3),
            pl.BlockSpec(sgu_b_full.shape, const3),
            pl.BlockSpec((None, None, MEM_LEN, MEM_WIDTH), lambda i: (layer, i // per_b, 0, 0)),
            pl.BlockSpec((None, None, MEM_HEADS * MEM_LEN, MEM_WIDTH), lambda i: (layer, i // per_b, 0, 0)),
        ],
        out_specs=[pl.BlockSpec((tm, SELF_WIDTH), row), pl.BlockSpec((tm, MEM_WIDTH), row)],
        out_shape=[jax.ShapeDtypeStruct((m, SELF_WIDTH), BF16),
                   jax.ShapeDtypeStruct((m, MEM_WIDTH), BF16)],
        compiler_params=_params("arbitrary"),
        name="sgu_prompt",
    )(x, w_bf, ln_g, ln_b, sgu_w, sgu_b_full, mk_bf, mvm_bf)


def _sgu_sample(x, w_bf, ln_g, ln_b, scale_row, shift_row):
    m = x.shape[0]
    c = lambda i: (0, 0)
    return pl.pallas_call(
        functools.partial(_sgu_body, prompt=False),
        grid=(1,),
        in_specs=[
            pl.BlockSpec((m, D_MODEL), c),
            pl.BlockSpec(w_bf.shape, c),
            pl.BlockSpec(ln_g.shape, c),
            pl.BlockSpec(ln_b.shape, c),
            pl.BlockSpec(scale_row.shape, c),
            pl.BlockSpec(shift_row.shape, c),
        ],
        out_specs=[pl.BlockSpec((m, SELF_WIDTH), c), pl.BlockSpec((m, MEM_WIDTH), c),
                   pl.BlockSpec((m, SELF_WIDTH), c)],
        out_shape=[jax.ShapeDtypeStruct((m, SELF_WIDTH), F32),
                   jax.ShapeDtypeStruct((m, MEM_WIDTH), F32),
                   jax.ShapeDtypeStruct((m, SELF_WIDTH), F32)],
        compiler_params=_params("arbitrary"),
        name="sgu_sample",
    )(x, w_bf, ln_g, ln_b, scale_row, shift_row)


def _out_ffn_body(x_ref, a_ref, m_ref, wo_ref, g1_ref, b1_ref, wu_ref, wd_ref, g2_ref, b2_ref,
                  o_ref):
    def mix_proj(rows):
        d = jnp.dot(a_ref[rows, :].astype(BF16), wo_ref[0:SELF_WIDTH, :],
                    preferred_element_type=F32)
        return d + jnp.dot(m_ref[rows, :].astype(BF16), wo_ref[SELF_WIDTH:, :],
                           preferred_element_type=F32)

    tm = x_ref.shape[0]
    sub = min(ROW_SUB_TILE, tm)
    tiles = [slice(i * sub, (i + 1) * sub) for i in range(tm // sub)]
    nxt = mix_proj(tiles[0])
    for i, rows in enumerate(tiles):
        d = nxt
        if i + 1 < len(tiles):
            nxt = mix_proj(tiles[i + 1])
        y = _ln(ALPHA * x_ref[rows, :] + d, g1_ref[...], b1_ref[...])
        yb = y.astype(BF16)
        acc = jnp.zeros_like(y)
        for c in range(D_FF // FF_CHUNK):
            cols = slice(c * FF_CHUNK, (c + 1) * FF_CHUNK)
            h = jnp.maximum(jnp.dot(yb, wu_ref[:, cols], preferred_element_type=F32), 0.0)
            acc = acc + jnp.dot((h * h).astype(BF16), wd_ref[cols, :],
                                preferred_element_type=F32)
        o_ref[rows, :] = _ln(ALPHA * y + acc, g2_ref[...], b2_ref[...])


def _out_ffn(x, a, ma, wo_bf, g1, b1, wu_bf, wd_bf, g2, b2, layer, tm, name):
    m = x.shape[0]
    row = lambda i: (i, 0)
    resident = lambda arr: pl.BlockSpec((None,) + arr.shape[1:], lambda i: (layer, 0, 0),
                                        pipeline_mode=pl.Buffered(1))
    return pl.pallas_call(
        _out_ffn_body,
        grid=(m // tm,),
        in_specs=[
            pl.BlockSpec((tm, D_MODEL), row),
            pl.BlockSpec((tm, SELF_WIDTH), row),
            pl.BlockSpec((tm, MEM_WIDTH), row),
            resident(wo_bf), resident(g1), resident(b1),
            resident(wu_bf), resident(wd_bf), resident(g2), resident(b2),
        ],
        out_specs=pl.BlockSpec((tm, D_MODEL), row),
        out_shape=jax.ShapeDtypeStruct((m, D_MODEL), F32),
        compiler_params=_params("arbitrary"),
        name=name,
    )(x, a, ma, wo_bf, g1, b1, wu_bf, wd_bf, g2, b2)


def kernel(x_prompt, x_sample, cache_k, cache_v, cache_mem_k, cache_mem_v, page_table, mem_prompt, w_in_attn, lambda_qk, subln_g, w_in_sgu, sgu_ln_g, sgu_ln_b, sgu_w, sgu_b, w_mem_kv, w_out, ln1_g, ln1_b, w_up, w_down, ln2_g, ln2_b):
    nb, seq, _ = x_prompt.shape
    db = x_sample.shape[0]
    row = lambda a: a.reshape(1, -1)

    xp = x_prompt.reshape(nb * seq, D_MODEL)
    xs = x_sample.reshape(db, D_MODEL)
    w_in_attn_bf = w_in_attn.astype(BF16)
    w_in_sgu_bf = w_in_sgu.astype(BF16)
    w_out_bf = w_out.astype(BF16)
    w_up_bf = w_up.astype(BF16)
    w_down_bf = w_down.astype(BF16)
    cache_k_pg = jnp.transpose(cache_k[0], (0, 2, 1, 3))
    cache_v_pg = jnp.transpose(cache_v[0], (0, 2, 1, 3))
    mem_kt = jnp.transpose(cache_mem_k, (0, 1, 3, 4, 2)).reshape(DEPTH, db, MEM_WIDTH, MEM_LEN)
    mem_vt = jnp.transpose(cache_mem_v, (0, 1, 3, 4, 2)).reshape(DEPTH, db, MEM_WIDTH, MEM_LEN)

    mkt_p, mvt_p, mk_bf, mvm_bf = _mem_kv(mem_prompt, w_mem_kv.astype(BF16))

    ln_rows = lambda a: a.reshape(DEPTH, 1, D_MODEL)
    ln1_g3, ln1_b3, ln2_g3, ln2_b3 = ln_rows(ln1_g), ln_rows(ln1_b), ln_rows(ln2_g), ln_rows(ln2_b)

    def out_ffn(x, a, ma, l, tm, name):
        return _out_ffn(x, a, ma, w_out_bf, ln1_g3, ln1_b3, w_up_bf, w_down_bf, ln2_g3, ln2_b3,
                        l, tm, name)

    lam0 = _lambda_init(0)
    g0 = row(subln_g[0])
    q_bf, k_pg, v_pg, k_bf, vt_bf, ma_p = _attn_proj_prompt(xp, w_in_attn_bf[0], mk_bf, mvm_bf, 0, seq)
    o_p = _diff_attn_prompt(q_bf, k_bf, vt_bf, lambda_qk[0], g0, lam0)
    xp = out_ffn(xp, o_p.reshape(nb * seq, SELF_WIDTH), ma_p, 0, FFN_TOKEN_TILE, "out_ffn_prompt0")

    q_s, k_s, v_s, qm_s = _attn_proj_sample(xs, w_in_attn_bf[0])
    o_s = _diff_attn_decode(q_s, k_s, v_s, cache_k_pg, cache_v_pg, page_table, lambda_qk[0], g0, lam0)
    ma_s = _mem_decode(qm_s, mem_kt, mem_vt, 0)
    xs = out_ffn(xs, o_s, ma_s, 0, db, "out_ffn_sample0")

    sgu_b_full = jnp.broadcast_to(sgu_b[0][:, :, None], (SGU_GROUPS, CHUNK, CHUNK))
    a_p, ma_p = _sgu_prompt(xp, w_in_sgu_bf[0], row(sgu_ln_g[0]), row(sgu_ln_b[0]), sgu_w[0],
                            sgu_b_full, mk_bf, mvm_bf, 1, seq)
    xp = out_ffn(xp, a_p, ma_p, 1, FFN_TOKEN_TILE, "out_ffn_prompt1")

    scale_row = row(jnp.repeat(sgu_w[0][:, 0, 0], CHUNK))
    shift_row = row(jnp.repeat(sgu_b[0][:, 0], CHUNK))
    a_s, qm_s, sv_s = _sgu_sample(xs, w_in_sgu_bf[0], row(sgu_ln_g[0]), row(sgu_ln_b[0]),
                                  scale_row, shift_row)
    ma_s = _mem_decode(qm_s, mem_kt, mem_vt, 1)
    xs = out_ffn(xs, a_s, ma_s, 1, db, "out_ffn_sample1")

    n_pg = seq // PAGE_SIZE
    paged = lambda a: jnp.transpose(
        a.reshape(1, nb, n_pg, DIFF_HEADS, PAGE_SIZE, HEAD_WIDTH), (0, 1, 2, 4, 3, 5))
    mem_out = lambda a: jnp.transpose(
        a.reshape(DEPTH, nb, MEM_HEADS, MEM_HEAD_DIM, MEM_LEN), (0, 1, 4, 2, 3))
    dec_shape = (1, db, 1, DIFF_HEADS, HEAD_WIDTH)
    return (xp.reshape(nb, seq, D_MODEL), xs.reshape(db, 1, D_MODEL),
            paged(k_pg), paged(v_pg), mem_out(mkt_p), mem_out(mvt_p),
            k_s.reshape(dec_shape), v_s.reshape(dec_shape), sv_s.reshape(dec_shape))
```

```python
import functools
import math

import jax
import jax.numpy as jnp
from jax import lax
from jax.experimental import pallas as pl
from jax.experimental.pallas import tpu as pltpu

F32 = jnp.float32
BF16 = jnp.bfloat16

D_MODEL = 1024
DEPTH = 2
PAGE_SIZE = 128
MEM_LEN = 256
MEM_HEADS = 4
MEM_HEAD_DIM = 64
MEM_WIDTH = MEM_HEADS * MEM_HEAD_DIM
SELF_WIDTH = D_MODEL - MEM_WIDTH
ATTN_IN = 3 * SELF_WIDTH + MEM_WIDTH
DIFF_HEAD_DIM = 64
DIFF_HEADS = SELF_WIDTH // (2 * DIFF_HEAD_DIM)
HEAD_WIDTH = 2 * DIFF_HEAD_DIM
CHUNK = 128
SGU_GROUPS = 6
D_FF = 4 * D_MODEL
ALPHA = (2 * DEPTH) ** 0.25
LN_EPS = 1e-5
NEG_INF = -1e30
QK_SCALE = DIFF_HEAD_DIM ** -0.5
QK_SCALE_LOG2 = QK_SCALE * math.log2(math.e)
MEM_SCALE = MEM_HEAD_DIM ** -0.5

LANES = 128
SUBLANES = 8
BF16_SUBLANES = 16
VT_ROWS = HEAD_WIDTH + BF16_SUBLANES
VMEM_LIMIT_BYTES = 56 * 1024 * 1024

TOKEN_TILE = 1024
FFN_TOKEN_TILE = 1024
ROW_SUB_TILE = 256
ATTN_BLOCK = 1024
FF_CHUNK = 1024
DECODE_PAGES_PER_STEP = 16
MEM_DECODE_ROWS_PER_STEP = 8

_NT = (((1,), (1,)), ((), ()))


def _lambda_init(layer):
    return 0.8 - 0.6 * math.exp(-0.3 * layer)


def _params(*sem):
    return pltpu.CompilerParams(dimension_semantics=sem, vmem_limit_bytes=VMEM_LIMIT_BYTES)


def _ln(x, g, b):
    mu = jnp.mean(x, axis=-1, keepdims=True)
    xc = x - mu
    var = jnp.mean(xc * xc, axis=-1, keepdims=True)
    return xc * lax.rsqrt(var + LN_EPS) * g + b


def _gelu_tanh(x):
    c = math.sqrt(2.0 / math.pi)
    return x * (0.5 * (1.0 + jnp.tanh(c * (x + 0.044715 * (x * x * x)))))


def _diff_lambda(lam_ref, lam0):
    l = lam_ref[...]
    a = jnp.sum(l[0:1, :] * l[1:2, :], axis=1, keepdims=True)
    b = jnp.sum(l[2:3, :] * l[3:4, :], axis=1, keepdims=True)
    return jnp.exp(a) - jnp.exp(b) + lam0


def _mem_kv_body(mem_ref, w_ref, mk_ref, mv_ref, mkb_ref, mvm_ref):
    kv = jnp.dot(mem_ref[...].astype(BF16), w_ref[...], preferred_element_type=F32)
    mk = kv[:, :MEM_WIDTH]
    mv = kv[:, MEM_WIDTH:]
    mk_ref[...] = mk.T
    mv_ref[...] = mv.T
    mkb_ref[...] = mk.astype(BF16)
    lane = lax.broadcasted_iota(jnp.int32, mv.shape, 1)
    for h in range(MEM_HEADS):
        mvm_ref[h * MEM_LEN:(h + 1) * MEM_LEN, :] = jnp.where(
            lane // MEM_HEAD_DIM == h, mv, 0.0).astype(BF16)


def _mem_kv(mem_prompt, w_mem_kv_bf):
    nb = mem_prompt.shape[0]
    per_lb = lambda l, b: (l, b, 0, 0)
    return pl.pallas_call(
        _mem_kv_body,
        grid=(DEPTH, nb),
        in_specs=[
            pl.BlockSpec((None, MEM_LEN, D_MODEL), lambda l, b: (b, 0, 0)),
            pl.BlockSpec((None, D_MODEL, 2 * MEM_WIDTH), lambda l, b: (l, 0, 0)),
        ],
        out_specs=[
            pl.BlockSpec((None, None, MEM_WIDTH, MEM_LEN), per_lb),
            pl.BlockSpec((None, None, MEM_WIDTH, MEM_LEN), per_lb),
            pl.BlockSpec((None, None, MEM_LEN, MEM_WIDTH), per_lb),
            pl.BlockSpec((None, None, MEM_HEADS * MEM_LEN, MEM_WIDTH), per_lb),
        ],
        out_shape=[
            jax.ShapeDtypeStruct((DEPTH, nb, MEM_WIDTH, MEM_LEN), F32),
            jax.ShapeDtypeStruct((DEPTH, nb, MEM_WIDTH, MEM_LEN), F32),
            jax.ShapeDtypeStruct((DEPTH, nb, MEM_LEN, MEM_WIDTH), BF16),
            jax.ShapeDtypeStruct((DEPTH, nb, MEM_HEADS * MEM_LEN, MEM_WIDTH), BF16),
        ],
        compiler_params=_params("arbitrary", "arbitrary"),
        name="mem_kv",
    )(mem_prompt, w_mem_kv_bf)


def _mem_attend_shared(qm, mk_ref, mvm_ref):
    lane = lax.broadcasted_iota(jnp.int32, qm.shape, 1)
    mk = mk_ref[...]
    ps = []
    for h in range(MEM_HEADS):
        qh = jnp.where(lane // MEM_HEAD_DIM == h, qm, 0.0).astype(BF16)
        s = lax.dot_general(qh, mk, _NT, preferred_element_type=F32) * MEM_SCALE
        m = jnp.max(s, axis=1, keepdims=True)
        e = jnp.exp(s - m)
        l = jnp.sum(e, axis=1, keepdims=True)
        ps.append((e * (1.0 / l)).astype(BF16))
    p = jnp.concatenate(ps, axis=1)
    return jnp.dot(p, mvm_ref[...], preferred_element_type=F32)


def _attn_proj_body(x_ref, w_ref, *refs, with_mem):
    if with_mem:
        mk_ref, mvm_ref, q_ref, kf_ref, vf_ref, kb_ref, vt_ref, m_ref = refs
    else:
        q_ref, kf_ref, vf_ref, m_ref = refs
    sw = SELF_WIDTH

    def project(rows):
        xb = x_ref[rows, :].astype(BF16)
        return tuple(jnp.dot(xb, w_ref[:, lo:hi], preferred_element_type=F32)
                     for lo, hi in ((0, sw), (sw, 2 * sw), (2 * sw, 3 * sw), (3 * sw, ATTN_IN)))

    if not with_mem:
        q, k, v, qm = project(slice(None))
        kf_ref[...] = k
        vf_ref[...] = v
        q_ref[...] = q
        m_ref[...] = qm
        return

    sub = ROW_SUB_TILE
    tiles = [slice(i * sub, (i + 1) * sub) for i in range(x_ref.shape[0] // sub)]
    nxt = project(tiles[0])
    for i, rows in enumerate(tiles):
        q, k, v, qm = nxt
        if i + 1 < len(tiles):
            nxt = project(tiles[i + 1])
        for h in range(DIFF_HEADS):
            cols = slice(h * HEAD_WIDTH, (h + 1) * HEAD_WIDTH)
            for p in range(sub // PAGE_SIZE):
                rr = slice(p * PAGE_SIZE, (p + 1) * PAGE_SIZE)
                page = rows.start // PAGE_SIZE + p
                kf_ref[page, h] = k[rr, cols]
                vf_ref[page, h] = v[rr, cols]
            q_ref[h, rows, :] = (q[:, cols] * QK_SCALE_LOG2).astype(BF16)
            kb_ref[h, rows, :] = k[:, cols].astype(BF16)
            vt_ref[h, 0:HEAD_WIDTH, rows] = v[:, cols].T.astype(BF16)
            vt_ref[h, HEAD_WIDTH:, rows] = jnp.ones((BF16_SUBLANES, sub), BF16)
        m_ref[rows, :] = _mem_attend_shared(qm, mk_ref, mvm_ref).astype(BF16)


def _attn_proj_prompt(x, w_bf, mk_bf, mvm_bf, layer, seq):
    m = x.shape[0]
    tm = TOKEN_TILE
    per_b = seq // tm
    pages = tm // PAGE_SIZE
    row = lambda i: (i, 0)
    nb = m // seq
    per_blk = ATTN_BLOCK // tm
    paged = pl.BlockSpec((pages, DIFF_HEADS, PAGE_SIZE, HEAD_WIDTH), lambda i: (i, 0, 0, 0))
    paged_shape = jax.ShapeDtypeStruct((m // PAGE_SIZE, DIFF_HEADS, PAGE_SIZE, HEAD_WIDTH), F32)
    heads = pl.BlockSpec((None, DIFF_HEADS, tm, HEAD_WIDTH), lambda i: (i // per_b, 0, i % per_b, 0))
    heads_shape = jax.ShapeDtypeStruct((nb, DIFF_HEADS, seq, HEAD_WIDTH), BF16)
    return pl.pallas_call(
        functools.partial(_attn_proj_body, with_mem=True),
        grid=(m // tm,),
        in_specs=[
            pl.BlockSpec((tm, D_MODEL), row),
            pl.BlockSpec(w_bf.shape, lambda i: (0, 0)),
            pl.BlockSpec((None, None, MEM_LEN, MEM_WIDTH), lambda i: (layer, i // per_b, 0, 0)),
            pl.BlockSpec((None, None, MEM_HEADS * MEM_LEN, MEM_WIDTH), lambda i: (layer, i // per_b, 0, 0)),
        ],
        out_specs=[
            heads,
            paged,
            paged,
            heads,
            pl.BlockSpec((None, DIFF_HEADS, None, VT_ROWS, tm),
                         lambda i: (i // per_b, 0, (i % per_b) // per_blk, 0, (i % per_b) % per_blk)),
            pl.BlockSpec((tm, MEM_WIDTH), row),
        ],
        out_shape=[
            heads_shape,
            paged_shape,
            paged_shape,
            heads_shape,
            jax.ShapeDtypeStruct((nb, DIFF_HEADS, seq // ATTN_BLOCK, VT_ROWS, ATTN_BLOCK), BF16),
            jax.ShapeDtypeStruct((m, MEM_WIDTH), BF16),
        ],
        compiler_params=_params("arbitrary"),
        name="attn_proj_prompt",
    )(x, w_bf, mk_bf, mvm_bf)


def _attn_proj_sample(x, w_bf):
    m = x.shape[0]
    row = lambda i: (0, 0)
    return pl.pallas_call(
        functools.partial(_attn_proj_body, with_mem=False),
        grid=(1,),
        in_specs=[pl.BlockSpec((m, D_MODEL), row), pl.BlockSpec(w_bf.shape, row)],
        out_specs=[
            pl.BlockSpec((m, SELF_WIDTH), row),
            pl.BlockSpec((m, SELF_WIDTH), row),
            pl.BlockSpec((m, SELF_WIDTH), row),
            pl.BlockSpec((m, MEM_WIDTH), row),
        ],
        out_shape=[
            jax.ShapeDtypeStruct((m, SELF_WIDTH), F32),
            jax.ShapeDtypeStruct((m, SELF_WIDTH), F32),
            jax.ShapeDtypeStruct((m, SELF_WIDTH), F32),
            jax.ShapeDtypeStruct((m, MEM_WIDTH), F32),
        ],
        compiler_params=_params("arbitrary"),
        name="attn_proj_sample",
    )(x, w_bf)


def _diff_attn_body(q_ref, k_ref, vt_ref, lam_ref, g_ref, o_ref,
                    qs_ref, m_ref, l_ref, acc_ref, sbuf_ref, *, blk, lam0):
    qi = pl.program_id(2)
    sq = blk // 2
    chains = [(1, 0), (1, 1), (0, 0), (0, 1)]

    def _init():
        q = q_ref[...]
        lane = lax.broadcasted_iota(jnp.int32, q.shape, 1)
        zero = jnp.zeros_like(q)
        qs_ref[0] = jnp.where(lane < DIFF_HEAD_DIM, q, zero)
        qs_ref[1] = jnp.where(lane >= DIFF_HEAD_DIM, q, zero)
        m_ref[...] = jnp.full(m_ref.shape, NEG_INF, F32)
        l_ref[...] = jnp.zeros(l_ref.shape, F32)
        acc_ref[...] = jnp.zeros(acc_ref.shape, F32)

    def cols(qb):
        return slice(qb * sq, (qb + 1) * sq)

    def scores(j, qb, c, nk=blk):
        k = k_ref[pl.ds(pl.multiple_of(j * blk, blk), nk), :]
        return lax.dot_general(k, qs_ref[c, cols(qb), :], _NT, preferred_element_type=F32)

    def causal(s, qb):
        key = lax.broadcasted_iota(jnp.int32, s.shape, 0)
        qry = qb * sq + lax.broadcasted_iota(jnp.int32, s.shape, 1)
        return jnp.where(key <= qry, s, NEG_INF)

    def chain_max(s, qb, c):
        m_prev = m_ref[c, 0:1, cols(qb)]
        m_new = jnp.maximum(m_prev, jnp.max(s, axis=0, keepdims=True))
        return m_new, jnp.exp2(m_prev - m_new)

    def reduce_chain(s, stats, vt, qb, c):
        m_new, alpha = stats
        p = jnp.exp2(s - m_new).astype(BF16)
        pv = jnp.dot(vt, p, preferred_element_type=F32)
        l_new = alpha * l_ref[c, 0:1, cols(qb)] + pv[HEAD_WIDTH:HEAD_WIDTH + 1, :]
        acc_ref[c, :, cols(qb)] = alpha * acc_ref[c, :, cols(qb)] + pv[0:HEAD_WIDTH, :]
        m_ref[c, :, cols(qb)] = jnp.broadcast_to(m_new, (SUBLANES, sq))
        l_ref[c, :, cols(qb)] = jnp.broadcast_to(l_new, (SUBLANES, sq))

    _init()
    sbuf_ref[0] = scores(0, *chains[0])
    sbuf_ref[1] = scores(0, *chains[1])

    def _plain_block(j):
        vt = vt_ref[j]
        s0, s1 = sbuf_ref[0], sbuf_ref[1]
        s2 = scores(j, *chains[2])
        st0 = chain_max(s0, *chains[0])
        st1 = chain_max(s1, *chains[1])
        reduce_chain(s0, st0, vt, *chains[0])
        s3 = scores(j, *chains[3])
        st2 = chain_max(s2, *chains[2])
        reduce_chain(s1, st1, vt, *chains[1])
        sbuf_ref[0] = scores(j + 1, *chains[0])
        st3 = chain_max(s3, *chains[3])
        reduce_chain(s2, st2, vt, *chains[2])
        sbuf_ref[1] = scores(j + 1, *chains[1])
        reduce_chain(s3, st3, vt, *chains[3])

    def _two_blocks(i, carry):
        _plain_block(2 * i)
        _plain_block(2 * i + 1)
        return carry

    def _last_odd_block(i, carry):
        _plain_block(qi - 1)
        return carry

    lax.fori_loop(0, lax.shift_right_logical(qi, 1), _two_blocks, 0)
    lax.fori_loop(0, qi & 1, _last_odd_block, 0)

    vt = vt_ref[qi]
    s0 = causal(sbuf_ref[0], chains[0][0])
    s1 = causal(sbuf_ref[1], chains[1][0])
    s2 = causal(scores(qi, *chains[2], nk=sq), chains[2][0])
    st0 = chain_max(s0, *chains[0])
    st1 = chain_max(s1, *chains[1])
    reduce_chain(s0, st0, vt, *chains[0])
    s3 = causal(scores(qi, *chains[3], nk=sq), chains[3][0])
    st2 = chain_max(s2, *chains[2])
    reduce_chain(s1, st1, vt, *chains[1])
    st3 = chain_max(s3, *chains[3])
    reduce_chain(s2, st2, vt[:, 0:sq], *chains[2])
    reduce_chain(s3, st3, vt[:, 0:sq], *chains[3])

    lam = _diff_lambda(lam_ref, lam0)
    inv0 = 1.0 / l_ref[0, 0:1, :]
    inv1 = 1.0 / l_ref[1, 0:1, :]
    ot = acc_ref[0] * inv0 - lam * (acc_ref[1] * inv1)
    ms = jnp.mean(ot * ot, axis=0, keepdims=True)
    yt = ot * lax.rsqrt(ms + LN_EPS)
    o_ref[...] = (yt.T * (g_ref[...] * (1.0 - lam0))).astype(o_ref.dtype)


def _diff_attn_prompt(q_bf, k_bf, vt_bf, lam_qk, g, lam0):
    nb, _, seq, _ = q_bf.shape
    blk = ATTN_BLOCK
    n_blk = seq // blk
    per_head = lambda b, h, i: (b, h, 0, 0)
    return pl.pallas_call(
        functools.partial(_diff_attn_body, blk=blk, lam0=lam0),
        grid=(nb, DIFF_HEADS, n_blk),
        in_specs=[
            pl.BlockSpec((None, None, blk, HEAD_WIDTH), lambda b, h, i: (b, h, i, 0)),
            pl.BlockSpec((None, None, seq, HEAD_WIDTH), per_head),
            pl.BlockSpec((None, None, n_blk, VT_ROWS, blk), lambda b, h, i: (b, h, 0, 0, 0)),
            pl.BlockSpec(lam_qk.shape, lambda b, h, i: (0, 0)),
            pl.BlockSpec(g.shape, lambda b, h, i: (0, 0)),
        ],
        out_specs=pl.BlockSpec((None, blk, HEAD_WIDTH), lambda b, h, i: (b, i, h)),
        out_shape=jax.ShapeDtypeStruct((nb, seq, SELF_WIDTH), BF16),
        scratch_shapes=[
            pltpu.VMEM((2, blk, HEAD_WIDTH), BF16),
            pltpu.VMEM((2, SUBLANES, blk), F32),
            pltpu.VMEM((2, SUBLANES, blk), F32),
            pltpu.VMEM((2, HEAD_WIDTH, blk), F32),
            pltpu.VMEM((2, blk, blk // 2), F32),
        ],
        compiler_params=_params("arbitrary", "arbitrary", "arbitrary"),
        name="diff_attn_prompt",
    )(q_bf, k_bf, vt_bf, lam_qk, g)


def _diff_attn_decode_body(pt_ref, q_ref, kn_ref, vn_ref, lam_ref, g_ref, *refs,
                           pages, lam0):
    k_refs = refs[:pages]
    v_refs = refs[pages:2 * pages]
    o_ref, qr_ref, m_ref, l_ref, acc_ref = refs[2 * pages:]
    j = pl.program_id(1)
    last = j == pl.num_programs(1) - 1
    lane = lax.broadcasted_iota(jnp.int32, (SUBLANES, HEAD_WIDTH), 1)
    row = lax.broadcasted_iota(jnp.int32, (SUBLANES, HEAD_WIDTH), 0)
    row1 = lax.broadcasted_iota(jnp.int32, (SUBLANES, 1), 0)

    def head_rows(h):
        return slice(h * SUBLANES, (h + 1) * SUBLANES)

    def own_row(ref, cols):
        b = pl.program_id(0)
        grp = ref[pl.ds(pl.multiple_of((b // SUBLANES) * SUBLANES, SUBLANES), SUBLANES), cols]
        return jnp.sum(jnp.where(row == b % SUBLANES, grp, 0.0), axis=0, keepdims=True)

    @pl.when(j == 0)
    def _init():
        for h in range(DIFF_HEADS):
            qh = own_row(q_ref, slice(h * HEAD_WIDTH, (h + 1) * HEAD_WIDTH)) * QK_SCALE
            qr_ref[head_rows(h), :] = jnp.where(lane // DIFF_HEAD_DIM == row, qh, 0.0)
        m_ref[...] = jnp.full(m_ref.shape, NEG_INF, F32)
        l_ref[...] = jnp.zeros(l_ref.shape, F32)
        acc_ref[...] = jnp.zeros(acc_ref.shape, F32)

    s = jnp.concatenate(
        [jnp.concatenate(
            [lax.dot_general(qr_ref[head_rows(h), :], k_refs[p][h], _NT,
                             preferred_element_type=F32) for p in range(pages)], axis=1)
         for h in range(DIFF_HEADS)], axis=0)
    m_prev = m_ref[:, 0:1]
    m_new = jnp.maximum(m_prev, jnp.max(s, axis=1, keepdims=True))
    alpha = jnp.exp(m_prev - m_new)
    e = jnp.exp(s - m_new)
    l_new = alpha * l_ref[:, 0:1] + jnp.sum(e, axis=1, keepdims=True)
    pv = []
    for h in range(DIFF_HEADS):
        acc_h = jnp.dot(e[head_rows(h), 0:PAGE_SIZE], v_refs[0][h], preferred_element_type=F32)
        for p in range(1, pages):
            acc_h = acc_h + jnp.dot(e[head_rows(h), p * PAGE_SIZE:(p + 1) * PAGE_SIZE],
                                    v_refs[p][h], preferred_element_type=F32)
        pv.append(acc_h)
    acc_ref[...] = alpha * acc_ref[...] + jnp.concatenate(pv, axis=0)
    m_ref[...] = jnp.broadcast_to(m_new, m_ref.shape)
    l_ref[...] = jnp.broadcast_to(l_new, l_ref.shape)

    @pl.when(last)
    def _finish():
        lam = _diff_lambda(lam_ref, lam0)
        scale = g_ref[...] * (1.0 - lam0)
        for h in range(DIFF_HEADS):
            cols = slice(h * HEAD_WIDTH, (h + 1) * HEAD_WIDTH)
            m_old = m_ref[head_rows(h), 0:1]
            s_new = jnp.sum(qr_ref[head_rows(h), :] * own_row(kn_ref, cols), axis=1, keepdims=True)
            m_fin = jnp.maximum(m_old, s_new)
            a_fin = jnp.exp(m_old - m_fin)
            p_new = jnp.exp(s_new - m_fin)
            l_fin = a_fin * l_ref[head_rows(h), 0:1] + p_new
            acc_fin = a_fin * acc_ref[head_rows(h), :] + p_new * own_row(vn_ref, cols)
            coef = jnp.where(row1 == 0, 1.0, -lam) / l_fin
            coef = jnp.where(row1 < 2, coef, 0.0)
            oh = jnp.sum(acc_fin * coef, axis=0, keepdims=True)
            ms = jnp.mean(oh * oh, axis=1, keepdims=True)
            o_ref[:, cols] = oh * lax.rsqrt(ms + LN_EPS) * scale


def _diff_attn_decode(q, k_new, v_new, cache_k, cache_v, page_table, lam_qk, g, lam0):
    db = q.shape[0]
    n_pages = page_table.shape[1]
    pages = DECODE_PAGES_PER_STEP
    assert n_pages % pages == 0
    pt = page_table.reshape(-1)
    vec = pl.BlockSpec((db, SELF_WIDTH), lambda b, j, pt: (0, 0))
    out_row = pl.BlockSpec((None, 1, SELF_WIDTH), lambda b, j, pt: (b, 0, 0))

    def page_spec(p):
        return pl.BlockSpec(
            (None, DIFF_HEADS, PAGE_SIZE, HEAD_WIDTH),
            lambda b, j, pt: (pt[b * n_pages + j * pages + p], 0, 0, 0))

    stat = pltpu.VMEM((DIFF_HEADS * SUBLANES, LANES), F32)
    grid_spec = pltpu.PrefetchScalarGridSpec(
        num_scalar_prefetch=1,
        grid=(db, n_pages // pages),
        in_specs=[vec, vec, vec,
                  pl.BlockSpec(lam_qk.shape, lambda b, j, pt: (0, 0)),
                  pl.BlockSpec(g.shape, lambda b, j, pt: (0, 0))]
                 + [page_spec(p) for p in range(pages)]
                 + [page_spec(p) for p in range(pages)],
        out_specs=out_row,
        scratch_shapes=[pltpu.VMEM((DIFF_HEADS * SUBLANES, HEAD_WIDTH), F32), stat, stat,
                        pltpu.VMEM((DIFF_HEADS * SUBLANES, HEAD_WIDTH), F32)],
    )
    out = pl.pallas_call(
        functools.partial(_diff_attn_decode_body, pages=pages, lam0=lam0),
        grid_spec=grid_spec,
        out_shape=jax.ShapeDtypeStruct((db, 1, SELF_WIDTH), F32),
        compiler_params=_params("arbitrary", "arbitrary"),
        name="diff_attn_decode",
    )(pt, q, k_new, v_new, lam_qk, g, *([cache_k] * pages), *([cache_v] * pages))
    return out.reshape(db, SELF_WIDTH)


def _mem_decode_body(qm_ref, mkt_ref, mvt_ref, o_ref):
    lane = lax.broadcasted_iota(jnp.int32, (SUBLANES, MEM_WIDTH), 1)
    row = lax.broadcasted_iota(jnp.int32, (SUBLANES, MEM_WIDTH), 0)
    own = lane // MEM_HEAD_DIM == row
    for r in range(qm_ref.shape[0]):
        qr = jnp.where(own, qm_ref[r:r + 1, :], 0.0)
        s = jnp.dot(qr, mkt_ref[r], preferred_element_type=F32) * MEM_SCALE
        m = jnp.max(s, axis=1, keepdims=True)
        e = jnp.exp(s - m)
        p = e * (1.0 / jnp.sum(e, axis=1, keepdims=True))
        o = lax.dot_general(p, mvt_ref[r], _NT, preferred_element_type=F32)
        o_ref[r:r + 1, :] = jnp.sum(jnp.where(own, o, 0.0), axis=0, keepdims=True)


def _mem_decode(qm, mem_kt, mem_vt, layer):
    db = qm.shape[0]
    rows = MEM_DECODE_ROWS_PER_STEP
    vec = pl.BlockSpec((rows, MEM_WIDTH), lambda b: (b, 0))
    mem = pl.BlockSpec((None, rows, MEM_WIDTH, MEM_LEN), lambda b: (layer, b, 0, 0))
    return pl.pallas_call(
        _mem_decode_body,
        grid=(db // rows,),
        in_specs=[vec, mem, mem],
        out_specs=vec,
        out_shape=jax.ShapeDtypeStruct((db, MEM_WIDTH), F32),
        compiler_params=_params("arbitrary"),
        name="mem_decode",
    )(qm, mem_kt, mem_vt)


def _sgu_body(x_ref, w_ref, lng_ref, lnb_ref, *refs, prompt):
    if prompt:
        ws_ref, sb_ref, mk_ref, mvm_ref, a_ref, m_ref = refs
    else:
        ws_ref, sb_ref, a_ref, m_ref, sv_ref = refs
    sw = SELF_WIDTH

    def project(rows):
        xb = x_ref[rows, :].astype(BF16)
        return (jnp.dot(xb, w_ref[:, 0:sw], preferred_element_type=F32),
                jnp.dot(xb, w_ref[:, sw:2 * sw], preferred_element_type=F32),
                jnp.dot(xb, w_ref[:, 2 * sw:], preferred_element_type=F32))

    if not prompt:
        up, vp, qm = project(slice(None))
        u = _gelu_tanh(up)
        v = _ln(_gelu_tanh(vp), lng_ref[...], lnb_ref[...])
        sv_ref[...] = v
        for g in range(SGU_GROUPS):
            cols = slice(g * CHUNK, (g + 1) * CHUNK)
            z = ws_ref[g, 0:1, 0:1] * v[:, cols] + sb_ref[g:g + 1, 0:1]
            a_ref[:, cols] = u[:, cols] * z
        m_ref[...] = qm
        return

    r = lax.broadcasted_iota(jnp.int32, (CHUNK, CHUNK), 0)
    c = lax.broadcasted_iota(jnp.int32, (CHUNK, CHUNK), 1)
    wms = [jnp.where(r >= c, ws_ref[g], 0.0).astype(BF16) for g in range(SGU_GROUPS)]
    sub = ROW_SUB_TILE
    tiles = [slice(i * sub, (i + 1) * sub) for i in range(x_ref.shape[0] // sub)]
    nxt = project(tiles[0])
    for i, rows in enumerate(tiles):
        up, vp, qm = nxt
        if i + 1 < len(tiles):
            nxt = project(tiles[i + 1])
        u = _gelu_tanh(up)
        vb = _ln(_gelu_tanh(vp), lng_ref[...], lnb_ref[...]).astype(BF16)
        for g in range(SGU_GROUPS):
            cols = slice(g * CHUNK, (g + 1) * CHUNK)
            for n in range(sub // CHUNK):
                rr = slice(n * CHUNK, (n + 1) * CHUNK)
                z = jnp.dot(wms[g], vb[rr, cols], preferred_element_type=F32) + sb_ref[g]
                a_ref[rows.start + n * CHUNK:rows.start + (n + 1) * CHUNK, cols] = (
                    u[rr, cols] * z).astype(BF16)
        m_ref[rows, :] = _mem_attend_shared(qm, mk_ref, mvm_ref).astype(BF16)


def _sgu_prompt(x, w_bf, ln_g, ln_b, sgu_w, sgu_b_full, mk_bf, mvm_bf, layer, seq):
    m = x.shape[0]
    tm = TOKEN_TILE
    per_b = seq // tm
    row = lambda i: (i, 0)
    const2 = lambda i: (0, 0)
    const3 = lambda i: (0, 0, 0)
    return pl.pallas_call(
        functools.partial(_sgu_body, prompt=True),
        grid=(m // tm,),
        in_specs=[
            pl.BlockSpec((tm, D_MODEL), row),
            pl.BlockSpec(w_bf.shape, const2),
            pl.BlockSpec(ln_g.shape, const2),
            pl.BlockSpec(ln_b.shape, const2),
            pl.BlockSpec(sgu_w.shape, const3),
            pl.BlockSpec(sgu_b_full.shape, const3),
            pl.BlockSpec((None, None, MEM_LEN, MEM_WIDTH), lambda i: (layer, i // per_b, 0, 0)),
            pl.BlockSpec((None, None, MEM_HEADS * MEM_LEN, MEM_WIDTH), lambda i: (layer, i // per_b, 0, 0)),
        ],
        out_specs=[pl.BlockSpec((tm, SELF_WIDTH), row), pl.BlockSpec((tm, MEM_WIDTH), row)],
        out_shape=[jax.ShapeDtypeStruct((m, SELF_WIDTH), BF16),
                   jax.ShapeDtypeStruct((m, MEM_WIDTH), BF16)],
        compiler_params=_params("arbitrary"),
        name="sgu_prompt",
    )(x, w_bf, ln_g, ln_b, sgu_w, sgu_b_full, mk_bf, mvm_bf)


def _sgu_sample(x, w_bf, ln_g, ln_b, sgu_w, sgu_b):
    m = x.shape[0]
    c = lambda i: (0, 0)
    return pl.pallas_call(
        functools.partial(_sgu_body, prompt=False),
        grid=(1,),
        in_specs=[
            pl.BlockSpec((m, D_MODEL), c),
            pl.BlockSpec(w_bf.shape, c),
            pl.BlockSpec(ln_g.shape, c),
            pl.BlockSpec(ln_b.shape, c),
            pl.BlockSpec(sgu_w.shape, lambda i: (0, 0, 0)),
            pl.BlockSpec(sgu_b.shape, c),
        ],
        out_specs=[pl.BlockSpec((m, SELF_WIDTH), c), pl.BlockSpec((m, MEM_WIDTH), c),
                   pl.BlockSpec((m, SELF_WIDTH), c)],
        out_shape=[jax.ShapeDtypeStruct((m, SELF_WIDTH), F32),
                   jax.ShapeDtypeStruct((m, MEM_WIDTH), F32),
                   jax.ShapeDtypeStruct((m, SELF_WIDTH), F32)],
        compiler_params=_params("arbitrary"),
        name="sgu_sample",
    )(x, w_bf, ln_g, ln_b, sgu_w, sgu_b)


def _out_ffn_body(x_ref, a_ref, m_ref, wo_ref, g1_ref, b1_ref, wu_ref, wd_ref, g2_ref, b2_ref,
                  o_ref, *, layer):
    ln_row = slice(layer, layer + 1)
    g1, b1, g2, b2 = (r[ln_row, :] for r in (g1_ref, b1_ref, g2_ref, b2_ref))

    def mix_proj(rows):
        d = jnp.dot(a_ref[rows, :].astype(BF16), wo_ref[0:SELF_WIDTH, :],
                    preferred_element_type=F32)
        return d + jnp.dot(m_ref[rows, :].astype(BF16), wo_ref[SELF_WIDTH:, :],
                           preferred_element_type=F32)

    tm = x_ref.shape[0]
    sub = min(ROW_SUB_TILE, tm)
    tiles = [slice(i * sub, (i + 1) * sub) for i in range(tm // sub)]
    nxt = mix_proj(tiles[0])
    for i, rows in enumerate(tiles):
        d = nxt
        if i + 1 < len(tiles):
            nxt = mix_proj(tiles[i + 1])
        y = _ln(ALPHA * x_ref[rows, :] + d, g1, b1)
        yb = y.astype(BF16)
        acc = jnp.zeros_like(y)
        for c in range(D_FF // FF_CHUNK):
            cols = slice(c * FF_CHUNK, (c + 1) * FF_CHUNK)
            h = jnp.maximum(jnp.dot(yb, wu_ref[:, cols], preferred_element_type=F32), 0.0)
            acc = acc + jnp.dot((h * h).astype(BF16), wd_ref[cols, :],
                                preferred_element_type=F32)
        o_ref[rows, :] = _ln(ALPHA * y + acc, g2, b2)


def _out_ffn(x, a, ma, wo_bf, g1, b1, wu_bf, wd_bf, g2, b2, layer, tm, name):
    m = x.shape[0]
    row = lambda i: (i, 0)
    resident = lambda arr: pl.BlockSpec((None,) + arr.shape[1:], lambda i: (layer, 0, 0),
                                        pipeline_mode=pl.Buffered(1))
    ln = lambda arr: pl.BlockSpec(arr.shape, lambda i: (0, 0), pipeline_mode=pl.Buffered(1))
    return pl.pallas_call(
        functools.partial(_out_ffn_body, layer=layer),
        grid=(m // tm,),
        in_specs=[
            pl.BlockSpec((tm, D_MODEL), row),
            pl.BlockSpec((tm, SELF_WIDTH), row),
            pl.BlockSpec((tm, MEM_WIDTH), row),
            resident(wo_bf), ln(g1), ln(b1),
            resident(wu_bf), resident(wd_bf), ln(g2), ln(b2),
        ],
        out_specs=pl.BlockSpec((tm, D_MODEL), row),
        out_shape=jax.ShapeDtypeStruct((m, D_MODEL), F32),
        compiler_params=_params("arbitrary"),
        name=name,
    )(x, a, ma, wo_bf, g1, b1, wu_bf, wd_bf, g2, b2)


def kernel(x_prompt, x_sample, cache_k, cache_v, cache_mem_k, cache_mem_v, page_table, mem_prompt, w_in_attn, lambda_qk, subln_g, w_in_sgu, sgu_ln_g, sgu_ln_b, sgu_w, sgu_b, w_mem_kv, w_out, ln1_g, ln1_b, w_up, w_down, ln2_g, ln2_b):
    nb, seq, _ = x_prompt.shape
    db = x_sample.shape[0]
    row = lambda a: a.reshape(1, -1)

    xp = x_prompt.reshape(nb * seq, D_MODEL)
    xs = x_sample.reshape(db, D_MODEL)
    w_in_attn_bf = w_in_attn.astype(BF16)
    w_in_sgu_bf = w_in_sgu.astype(BF16)
    w_out_bf = w_out.astype(BF16)
    w_up_bf = w_up.astype(BF16)
    w_down_bf = w_down.astype(BF16)
    cache_k_pg = jnp.transpose(cache_k[0], (0, 2, 1, 3))
    cache_v_pg = jnp.transpose(cache_v[0], (0, 2, 1, 3))
    mem_kt = jnp.transpose(cache_mem_k, (0, 1, 3, 4, 2)).reshape(DEPTH, db, MEM_WIDTH, MEM_LEN)
    mem_vt = jnp.transpose(cache_mem_v, (0, 1, 3, 4, 2)).reshape(DEPTH, db, MEM_WIDTH, MEM_LEN)

    mkt_p, mvt_p, mk_bf, mvm_bf = _mem_kv(mem_prompt, w_mem_kv.astype(BF16))

    def out_ffn(x, a, ma, l, tm, name):
        return _out_ffn(x, a, ma, w_out_bf, ln1_g, ln1_b, w_up_bf, w_down_bf, ln2_g, ln2_b,
                        l, tm, name)

    lam0 = _lambda_init(0)
    g0 = row(subln_g[0])
    q_bf, k_pg, v_pg, k_bf, vt_bf, ma_p = _attn_proj_prompt(xp, w_in_attn_bf[0], mk_bf, mvm_bf, 0, seq)
    o_p = _diff_attn_prompt(q_bf, k_bf, vt_bf, lambda_qk[0], g0, lam0)
    xp = out_ffn(xp, o_p.reshape(nb * seq, SELF_WIDTH), ma_p, 0, FFN_TOKEN_TILE, "out_ffn_prompt0")

    q_s, k_s, v_s, qm_s = _attn_proj_sample(xs, w_in_attn_bf[0])
    o_s = _diff_attn_decode(q_s, k_s, v_s, cache_k_pg, cache_v_pg, page_table, lambda_qk[0], g0, lam0)
    ma_s = _mem_decode(qm_s, mem_kt, mem_vt, 0)
    xs = out_ffn(xs, o_s, ma_s, 0, db, "out_ffn_sample0")

    sgu_b_full = jnp.broadcast_to(sgu_b[0][:, :, None], (SGU_GROUPS, CHUNK, CHUNK))
    a_p, ma_p = _sgu_prompt(xp, w_in_sgu_bf[0], row(sgu_ln_g[0]), row(sgu_ln_b[0]), sgu_w[0],
                            sgu_b_full, mk_bf, mvm_bf, 1, seq)
    xp = out_ffn(xp, a_p, ma_p, 1, FFN_TOKEN_TILE, "out_ffn_prompt1")

    a_s, qm_s, sv_s = _sgu_sample(xs, w_in_sgu_bf[0], row(sgu_ln_g[0]), row(sgu_ln_b[0]),
                                  sgu_w[0], sgu_b[0])
    ma_s = _mem_decode(qm_s, mem_kt, mem_vt, 1)
    xs = out_ffn(xs, a_s, ma_s, 1, db, "out_ffn_sample1")

    n_pg = seq // PAGE_SIZE
    paged = lambda a: jnp.transpose(
        a.reshape(1, nb, n_pg, DIFF_HEADS, PAGE_SIZE, HEAD_WIDTH), (0, 1, 2, 4, 3, 5))
    mem_out = lambda a: jnp.transpose(
        a.reshape(DEPTH, nb, MEM_HEADS, MEM_HEAD_DIM, MEM_LEN), (0, 1, 4, 2, 3))
    dec_shape = (1, db, 1, DIFF_HEADS, HEAD_WIDTH)
    return (xp.reshape(nb, seq, D_MODEL), xs.reshape(db, 1, D_MODEL),
            paged(k_pg), paged(v_pg), mem_out(mkt_p), mem_out(mvt_p),
            k_s.reshape(dec_shape), v_s.reshape(dec_shape), sv_s.reshape(dec_shape))
```

```python
import functools
import math

import jax
import jax.numpy as jnp
from jax import lax
from jax.experimental import pallas as pl
from jax.experimental.pallas import tpu as pltpu

F32 = jnp.float32
BF16 = jnp.bfloat16

D_MODEL = 1024
DEPTH = 2
PAGE_SIZE = 128
MEM_LEN = 256
MEM_HEADS = 4
MEM_HEAD_DIM = 64
MEM_WIDTH = MEM_HEADS * MEM_HEAD_DIM
SELF_WIDTH = D_MODEL - MEM_WIDTH
ATTN_IN = 3 * SELF_WIDTH + MEM_WIDTH
DIFF_HEAD_DIM = 64
DIFF_HEADS = SELF_WIDTH // (2 * DIFF_HEAD_DIM)
HEAD_WIDTH = 2 * DIFF_HEAD_DIM
CHUNK = 128
SGU_GROUPS = 6
D_FF = 4 * D_MODEL
ALPHA = (2 * DEPTH) ** 0.25
LN_EPS = 1e-5
NEG_INF = -1e30
QK_SCALE = DIFF_HEAD_DIM ** -0.5
QK_SCALE_LOG2 = QK_SCALE * math.log2(math.e)
MEM_SCALE = MEM_HEAD_DIM ** -0.5

LANES = 128
SUBLANES = 8
BF16_SUBLANES = 16
VT_ROWS = HEAD_WIDTH + BF16_SUBLANES
VMEM_LIMIT_BYTES = 56 * 1024 * 1024

TOKEN_TILE = 1024
FFN_TOKEN_TILE = 1024
ROW_SUB_TILE = 256
ATTN_BLOCK = 1024
FF_CHUNK = 1024
DECODE_PAGES_PER_STEP = 16
DECODE_MAPS = 16
MEM_DECODE_ROWS_PER_STEP = 8

_NT = (((1,), (1,)), ((), ()))


def _lambda_init(layer):
    return 0.8 - 0.6 * math.exp(-0.3 * layer)


def _params(*sem):
    return pltpu.CompilerParams(dimension_semantics=sem, vmem_limit_bytes=VMEM_LIMIT_BYTES)


def _ln(x, g, b):
    mu = jnp.mean(x, axis=-1, keepdims=True)
    xc = x - mu
    var = jnp.mean(xc * xc, axis=-1, keepdims=True)
    return xc * lax.rsqrt(var + LN_EPS) * g + b


def _gelu_tanh(x):
    c = math.sqrt(2.0 / math.pi)
    return x * (0.5 * (1.0 + jnp.tanh(c * (x + 0.044715 * (x * x * x)))))


def _diff_lambda(lam_ref, lam0):
    l = lam_ref[...]
    a = jnp.sum(l[0:1, :] * l[1:2, :], axis=1, keepdims=True)
    b = jnp.sum(l[2:3, :] * l[3:4, :], axis=1, keepdims=True)
    return jnp.exp(a) - jnp.exp(b) + lam0


def _mem_kv_body(mem_ref, w_ref, mk_ref, mv_ref, mkb_ref, mvm_ref):
    kv = jnp.dot(mem_ref[...].astype(BF16), w_ref[...], preferred_element_type=F32)
    mk = kv[:, :MEM_WIDTH]
    mv = kv[:, MEM_WIDTH:]
    mk_ref[...] = mk.T
    mv_ref[...] = mv.T
    mkb_ref[...] = mk.astype(BF16)
    lane = lax.broadcasted_iota(jnp.int32, mv.shape, 1)
    for h in range(MEM_HEADS):
        mvm_ref[h * MEM_LEN:(h + 1) * MEM_LEN, :] = jnp.where(
            lane // MEM_HEAD_DIM == h, mv, 0.0).astype(BF16)


def _mem_kv(mem_prompt, w_mem_kv_bf):
    nb = mem_prompt.shape[0]
    per_lb = lambda l, b: (l, b, 0, 0)
    return pl.pallas_call(
        _mem_kv_body,
        grid=(DEPTH, nb),
        in_specs=[
            pl.BlockSpec((None, MEM_LEN, D_MODEL), lambda l, b: (b, 0, 0)),
            pl.BlockSpec((None, D_MODEL, 2 * MEM_WIDTH), lambda l, b: (l, 0, 0)),
        ],
        out_specs=[
            pl.BlockSpec((None, None, MEM_WIDTH, MEM_LEN), per_lb),
            pl.BlockSpec((None, None, MEM_WIDTH, MEM_LEN), per_lb),
            pl.BlockSpec((None, None, MEM_LEN, MEM_WIDTH), per_lb),
            pl.BlockSpec((None, None, MEM_HEADS * MEM_LEN, MEM_WIDTH), per_lb),
        ],
        out_shape=[
            jax.ShapeDtypeStruct((DEPTH, nb, MEM_WIDTH, MEM_LEN), F32),
            jax.ShapeDtypeStruct((DEPTH, nb, MEM_WIDTH, MEM_LEN), F32),
            jax.ShapeDtypeStruct((DEPTH, nb, MEM_LEN, MEM_WIDTH), BF16),
            jax.ShapeDtypeStruct((DEPTH, nb, MEM_HEADS * MEM_LEN, MEM_WIDTH), BF16),
        ],
        compiler_params=_params("arbitrary", "arbitrary"),
        name="mem_kv",
    )(mem_prompt, w_mem_kv_bf)


def _mem_attend_shared(qm, mk_ref, mvm_ref):
    lane = lax.broadcasted_iota(jnp.int32, qm.shape, 1)
    mk = mk_ref[...]
    ps = []
    for h in range(MEM_HEADS):
        qh = jnp.where(lane // MEM_HEAD_DIM == h, qm, 0.0).astype(BF16)
        s = lax.dot_general(qh, mk, _NT, preferred_element_type=F32) * MEM_SCALE
        m = jnp.max(s, axis=1, keepdims=True)
        e = jnp.exp(s - m)
        l = jnp.sum(e, axis=1, keepdims=True)
        ps.append((e * (1.0 / l)).astype(BF16))
    p = jnp.concatenate(ps, axis=1)
    return jnp.dot(p, mvm_ref[...], preferred_element_type=F32)


def _attn_proj_body(x_ref, w_ref, *refs, with_mem):
    if with_mem:
        mk_ref, mvm_ref, q_ref, kf_ref, vf_ref, kb_ref, vt_ref, m_ref = refs
    else:
        q_ref, kf_ref, vf_ref, m_ref = refs
    sw = SELF_WIDTH

    def project(rows):
        xb = x_ref[rows, :].astype(BF16)
        return tuple(jnp.dot(xb, w_ref[:, lo:hi], preferred_element_type=F32)
                     for lo, hi in ((0, sw), (sw, 2 * sw), (2 * sw, 3 * sw), (3 * sw, ATTN_IN)))

    if not with_mem:
        q, k, v, qm = project(slice(None))
        kf_ref[...] = k
        vf_ref[...] = v
        q_ref[...] = q
        m_ref[...] = qm
        return

    sub = ROW_SUB_TILE
    tiles = [slice(i * sub, (i + 1) * sub) for i in range(x_ref.shape[0] // sub)]
    nxt = project(tiles[0])
    for i, rows in enumerate(tiles):
        q, k, v, qm = nxt
        if i + 1 < len(tiles):
            nxt = project(tiles[i + 1])
        for h in range(DIFF_HEADS):
            cols = slice(h * HEAD_WIDTH, (h + 1) * HEAD_WIDTH)
            for p in range(sub // PAGE_SIZE):
                rr = slice(p * PAGE_SIZE, (p + 1) * PAGE_SIZE)
                page = rows.start // PAGE_SIZE + p
                kf_ref[page, h] = k[rr, cols]
                vf_ref[page, h] = v[rr, cols]
            q_ref[h, rows, :] = (q[:, cols] * QK_SCALE_LOG2).astype(BF16)
            kb_ref[h, rows, :] = k[:, cols].astype(BF16)
            vt_ref[h, 0:HEAD_WIDTH, rows] = v[:, cols].T.astype(BF16)
            vt_ref[h, HEAD_WIDTH:, rows] = jnp.ones((BF16_SUBLANES, sub), BF16)
        m_ref[rows, :] = _mem_attend_shared(qm, mk_ref, mvm_ref).astype(BF16)


def _attn_proj_prompt(x, w_bf, mk_bf, mvm_bf, layer, seq):
    m = x.shape[0]
    tm = TOKEN_TILE
    per_b = seq // tm
    pages = tm // PAGE_SIZE
    row = lambda i: (i, 0)
    nb = m // seq
    per_blk = ATTN_BLOCK // tm
    paged = pl.BlockSpec((pages, DIFF_HEADS, PAGE_SIZE, HEAD_WIDTH), lambda i: (i, 0, 0, 0))
    paged_shape = jax.ShapeDtypeStruct((m // PAGE_SIZE, DIFF_HEADS, PAGE_SIZE, HEAD_WIDTH), F32)
    heads = pl.BlockSpec((None, DIFF_HEADS, tm, HEAD_WIDTH), lambda i: (i // per_b, 0, i % per_b, 0))
    heads_shape = jax.ShapeDtypeStruct((nb, DIFF_HEADS, seq, HEAD_WIDTH), BF16)
    return pl.pallas_call(
        functools.partial(_attn_proj_body, with_mem=True),
        grid=(m // tm,),
        in_specs=[
            pl.BlockSpec((tm, D_MODEL), row),
            pl.BlockSpec(w_bf.shape, lambda i: (0, 0)),
            pl.BlockSpec((None, None, MEM_LEN, MEM_WIDTH), lambda i: (layer, i // per_b, 0, 0)),
            pl.BlockSpec((None, None, MEM_HEADS * MEM_LEN, MEM_WIDTH), lambda i: (layer, i // per_b, 0, 0)),
        ],
        out_specs=[
            heads,
            paged,
            paged,
            heads,
            pl.BlockSpec((None, DIFF_HEADS, None, VT_ROWS, tm),
                         lambda i: (i // per_b, 0, (i % per_b) // per_blk, 0, (i % per_b) % per_blk)),
            pl.BlockSpec((tm, MEM_WIDTH), row),
        ],
        out_shape=[
            heads_shape,
            paged_shape,
            paged_shape,
            heads_shape,
            jax.ShapeDtypeStruct((nb, DIFF_HEADS, seq // ATTN_BLOCK, VT_ROWS, ATTN_BLOCK), BF16),
            jax.ShapeDtypeStruct((m, MEM_WIDTH), BF16),
        ],
        compiler_params=_params("arbitrary"),
        name="attn_proj_prompt",
    )(x, w_bf, mk_bf, mvm_bf)


def _attn_proj_sample(x, w_bf):
    m = x.shape[0]
    row = lambda i: (0, 0)
    return pl.pallas_call(
        functools.partial(_attn_proj_body, with_mem=False),
        grid=(1,),
        in_specs=[pl.BlockSpec((m, D_MODEL), row), pl.BlockSpec(w_bf.shape, row)],
        out_specs=[
            pl.BlockSpec((m, SELF_WIDTH), row),
            pl.BlockSpec((m, SELF_WIDTH), row),
            pl.BlockSpec((m, SELF_WIDTH), row),
            pl.BlockSpec((m, MEM_WIDTH), row),
        ],
        out_shape=[
            jax.ShapeDtypeStruct((m, SELF_WIDTH), F32),
            jax.ShapeDtypeStruct((m, SELF_WIDTH), F32),
            jax.ShapeDtypeStruct((m, SELF_WIDTH), F32),
            jax.ShapeDtypeStruct((m, MEM_WIDTH), F32),
        ],
        compiler_params=_params("arbitrary"),
        name="attn_proj_sample",
    )(x, w_bf)


def _diff_attn_body(q_ref, k_ref, vt_ref, lam_ref, g_ref, o_ref,
                    qs_ref, m_ref, l_ref, acc_ref, sbuf_ref, *, blk, lam0):
    qi = pl.program_id(2)
    sq = blk // 2
    chains = [(1, 0), (1, 1), (0, 0), (0, 1)]

    def _init():
        q = q_ref[...]
        lane = lax.broadcasted_iota(jnp.int32, q.shape, 1)
        zero = jnp.zeros_like(q)
        qs_ref[0] = jnp.where(lane < DIFF_HEAD_DIM, q, zero)
        qs_ref[1] = jnp.where(lane >= DIFF_HEAD_DIM, q, zero)
        m_ref[...] = jnp.full(m_ref.shape, NEG_INF, F32)
        l_ref[...] = jnp.zeros(l_ref.shape, F32)
        acc_ref[...] = jnp.zeros(acc_ref.shape, F32)

    def cols(qb):
        return slice(qb * sq, (qb + 1) * sq)

    def scores(j, qb, c, nk=blk):
        k = k_ref[pl.ds(pl.multiple_of(j * blk, blk), nk), :]
        return lax.dot_general(k, qs_ref[c, cols(qb), :], _NT, preferred_element_type=F32)

    def causal(s, qb):
        key = lax.broadcasted_iota(jnp.int32, s.shape, 0)
        qry = qb * sq + lax.broadcasted_iota(jnp.int32, s.shape, 1)
        return jnp.where(key <= qry, s, NEG_INF)

    def chain_max(s, qb, c):
        m_prev = m_ref[c, 0:1, cols(qb)]
        m_new = jnp.maximum(m_prev, jnp.max(s, axis=0, keepdims=True))
        return m_new, jnp.exp2(m_prev - m_new)

    def reduce_chain(s, stats, vt, qb, c):
        m_new, alpha = stats
        p = jnp.exp2(s - m_new).astype(BF16)
        pv = jnp.dot(vt, p, preferred_element_type=F32)
        l_new = alpha * l_ref[c, 0:1, cols(qb)] + pv[HEAD_WIDTH:HEAD_WIDTH + 1, :]
        acc_ref[c, :, cols(qb)] = alpha * acc_ref[c, :, cols(qb)] + pv[0:HEAD_WIDTH, :]
        m_ref[c, :, cols(qb)] = jnp.broadcast_to(m_new, (SUBLANES, sq))
        l_ref[c, :, cols(qb)] = jnp.broadcast_to(l_new, (SUBLANES, sq))

    _init()
    sbuf_ref[0] = scores(0, *chains[0])
    sbuf_ref[1] = scores(0, *chains[1])

    def _plain_block(j):
        vt = vt_ref[j]
        s0, s1 = sbuf_ref[0], sbuf_ref[1]
        s2 = scores(j, *chains[2])
        st0 = chain_max(s0, *chains[0])
        st1 = chain_max(s1, *chains[1])
        reduce_chain(s0, st0, vt, *chains[0])
        s3 = scores(j, *chains[3])
        st2 = chain_max(s2, *chains[2])
        reduce_chain(s1, st1, vt, *chains[1])
        sbuf_ref[0] = scores(j + 1, *chains[0])
        st3 = chain_max(s3, *chains[3])
        reduce_chain(s2, st2, vt, *chains[2])
        sbuf_ref[1] = scores(j + 1, *chains[1])
        reduce_chain(s3, st3, vt, *chains[3])

    def _two_blocks(i, carry):
        _plain_block(2 * i)
        _plain_block(2 * i + 1)
        return carry

    def _last_odd_block(i, carry):
        _plain_block(qi - 1)
        return carry

    lax.fori_loop(0, lax.shift_right_logical(qi, 1), _two_blocks, 0)
    lax.fori_loop(0, qi & 1, _last_odd_block, 0)

    vt = vt_ref[qi]
    s0 = causal(sbuf_ref[0], chains[0][0])
    s1 = causal(sbuf_ref[1], chains[1][0])
    s2 = causal(scores(qi, *chains[2], nk=sq), chains[2][0])
    st0 = chain_max(s0, *chains[0])
    st1 = chain_max(s1, *chains[1])
    reduce_chain(s0, st0, vt, *chains[0])
    s3 = causal(scores(qi, *chains[3], nk=sq), chains[3][0])
    st2 = chain_max(s2, *chains[2])
    reduce_chain(s1, st1, vt, *chains[1])
    st3 = chain_max(s3, *chains[3])
    reduce_chain(s2, st2, vt[:, 0:sq], *chains[2])
    reduce_chain(s3, st3, vt[:, 0:sq], *chains[3])

    lam = _diff_lambda(lam_ref, lam0)
    inv0 = 1.0 / l_ref[0, 0:1, :]
    inv1 = 1.0 / l_ref[1, 0:1, :]
    ot = acc_ref[0] * inv0 - lam * (acc_ref[1] * inv1)
    ms = jnp.mean(ot * ot, axis=0, keepdims=True)
    yt = ot * lax.rsqrt(ms + LN_EPS)
    o_ref[...] = (yt.T * (g_ref[...] * (1.0 - lam0))).astype(o_ref.dtype)


def _diff_attn_prompt(q_bf, k_bf, vt_bf, lam_qk, g, lam0):
    nb, _, seq, _ = q_bf.shape
    blk = ATTN_BLOCK
    n_blk = seq // blk
    per_head = lambda b, h, i: (b, h, 0, 0)
    return pl.pallas_call(
        functools.partial(_diff_attn_body, blk=blk, lam0=lam0),
        grid=(nb, DIFF_HEADS, n_blk),
        in_specs=[
            pl.BlockSpec((None, None, blk, HEAD_WIDTH), lambda b, h, i: (b, h, i, 0)),
            pl.BlockSpec((None, None, seq, HEAD_WIDTH), per_head),
            pl.BlockSpec((None, None, n_blk, VT_ROWS, blk), lambda b, h, i: (b, h, 0, 0, 0)),
            pl.BlockSpec(lam_qk.shape, lambda b, h, i: (0, 0)),
            pl.BlockSpec(g.shape, lambda b, h, i: (0, 0)),
        ],
        out_specs=pl.BlockSpec((None, blk, HEAD_WIDTH), lambda b, h, i: (b, i, h)),
        out_shape=jax.ShapeDtypeStruct((nb, seq, SELF_WIDTH), BF16),
        scratch_shapes=[
            pltpu.VMEM((2, blk, HEAD_WIDTH), BF16),
            pltpu.VMEM((2, SUBLANES, blk), F32),
            pltpu.VMEM((2, SUBLANES, blk), F32),
            pltpu.VMEM((2, HEAD_WIDTH, blk), F32),
            pltpu.VMEM((2, blk, blk // 2), F32),
        ],
        compiler_params=_params("arbitrary", "arbitrary", "arbitrary"),
        name="diff_attn_prompt",
    )(q_bf, k_bf, vt_bf, lam_qk, g)


def _diff_attn_decode_body(pt_ref, q_ref, kn_ref, vn_ref, lam_ref, g_ref, *refs,
                           pages, lam0):
    k_refs = refs[:pages]
    v_refs = refs[pages:2 * pages]
    o_ref, wq_ref, m_ref, l_ref, acc_ref = refs[2 * pages:]
    j = pl.program_id(1)
    last = j == pl.num_programs(1) - 1
    maps = DECODE_MAPS
    _TN = (((0,), (0,)), ((), ()))
    eye = (lax.broadcasted_iota(jnp.int32, (maps, maps), 0)
           == lax.broadcasted_iota(jnp.int32, (maps, maps), 1))

    def column(x_row):
        return jnp.sum(jnp.where(eye, jnp.broadcast_to(x_row, (maps, maps)), 0.0),
                       axis=1, keepdims=True)

    def own_row(ref):
        b = pl.program_id(0)
        grp = ref[pl.ds(pl.multiple_of((b // SUBLANES) * SUBLANES, SUBLANES), SUBLANES), :]
        sub = lax.broadcasted_iota(jnp.int32, grp.shape, 0)
        return jnp.sum(jnp.where(sub == b % SUBLANES, grp, 0.0), axis=0, keepdims=True)

    def page(refs_, p):
        return jnp.concatenate([refs_[p][h] for h in range(DIFF_HEADS)], axis=1)

    @pl.when(j == 0)
    def _init():
        lane = lax.broadcasted_iota(jnp.int32, wq_ref.shape, 1)
        r = lax.broadcasted_iota(jnp.int32, wq_ref.shape, 0)
        wq_ref[...] = jnp.where(lane // DIFF_HEAD_DIM == r, own_row(q_ref) * QK_SCALE, 0.0)
        m_ref[...] = jnp.full(m_ref.shape, NEG_INF, F32)
        l_ref[...] = jnp.zeros(l_ref.shape, F32)
        acc_ref[...] = jnp.zeros(acc_ref.shape, F32)

    wq = wq_ref[...]
    s = jnp.concatenate(
        [lax.dot_general(page(k_refs, p), wq, _NT, preferred_element_type=F32)
         for p in range(pages)], axis=0)
    m_prev = m_ref[0:1, :]
    m_new = jnp.maximum(m_prev, jnp.max(s, axis=0, keepdims=True))
    alpha = jnp.exp(m_prev - m_new)
    e = jnp.exp(s - m_new)
    l_new = alpha * l_ref[0:1, :] + jnp.sum(e, axis=0, keepdims=True)
    acc = column(alpha) * acc_ref[...]
    for p in range(pages):
        acc = acc + lax.dot_general(e[p * PAGE_SIZE:(p + 1) * PAGE_SIZE, :], page(v_refs, p), _TN,
                                    preferred_element_type=F32)
    acc_ref[...] = acc
    m_ref[...] = jnp.broadcast_to(m_new, m_ref.shape)
    l_ref[...] = jnp.broadcast_to(l_new, l_ref.shape)

    @pl.when(last)
    def _finish():
        lam = _diff_lambda(lam_ref, lam0)
        k_new = jnp.broadcast_to(own_row(kn_ref), (SUBLANES, SELF_WIDTH))
        s_new = lax.dot_general(k_new, wq, _NT, preferred_element_type=F32)[0:1, :]
        m_fin = jnp.maximum(m_new, s_new)
        a_fin = jnp.exp(m_new - m_fin)
        p_new = jnp.exp(s_new - m_fin)
        l_fin = a_fin * l_new + p_new
        r_row = lax.broadcasted_iota(jnp.int32, (1, maps), 1)
        coef = jnp.where(r_row % 2 == 0, 1.0, -lam) / l_fin
        coef = jnp.where(r_row < 2 * DIFF_HEADS, coef, 0.0)
        acc_fin = column(a_fin) * acc + column(p_new) * own_row(vn_ref)
        lane = lax.broadcasted_iota(jnp.int32, acc_fin.shape, 1)
        r = lax.broadcasted_iota(jnp.int32, acc_fin.shape, 0)
        own = lane // HEAD_WIDTH == r // 2
        o = jnp.sum(jnp.where(own, acc_fin * column(coef), 0.0), axis=0, keepdims=True)
        scale = g_ref[...] * (1.0 - lam0)
        for h in range(DIFF_HEADS):
            cols = slice(h * HEAD_WIDTH, (h + 1) * HEAD_WIDTH)
            oh = o[:, cols]
            ms = jnp.mean(oh * oh, axis=1, keepdims=True)
            o_ref[:, cols] = oh * lax.rsqrt(ms + LN_EPS) * scale


def _diff_attn_decode(q, k_new, v_new, cache_k, cache_v, page_table, lam_qk, g, lam0):
    db = q.shape[0]
    n_pages = page_table.shape[1]
    pages = DECODE_PAGES_PER_STEP
    assert n_pages % pages == 0
    pt = page_table.reshape(-1)
    vec = pl.BlockSpec((db, SELF_WIDTH), lambda b, j, pt: (0, 0))
    out_row = pl.BlockSpec((None, 1, SELF_WIDTH), lambda b, j, pt: (b, 0, 0))

    def page_spec(p):
        return pl.BlockSpec(
            (None, DIFF_HEADS, PAGE_SIZE, HEAD_WIDTH),
            lambda b, j, pt: (pt[b * n_pages + j * pages + p], 0, 0, 0))

    stat = pltpu.VMEM((SUBLANES, DECODE_MAPS), F32)
    grid_spec = pltpu.PrefetchScalarGridSpec(
        num_scalar_prefetch=1,
        grid=(db, n_pages // pages),
        in_specs=[vec, vec, vec,
                  pl.BlockSpec(lam_qk.shape, lambda b, j, pt: (0, 0)),
                  pl.BlockSpec(g.shape, lambda b, j, pt: (0, 0))]
                 + [page_spec(p) for p in range(pages)]
                 + [page_spec(p) for p in range(pages)],
        out_specs=out_row,
        scratch_shapes=[pltpu.VMEM((DECODE_MAPS, SELF_WIDTH), F32), stat, stat,
                        pltpu.VMEM((DECODE_MAPS, SELF_WIDTH), F32)],
    )
    out = pl.pallas_call(
        functools.partial(_diff_attn_decode_body, pages=pages, lam0=lam0),
        grid_spec=grid_spec,
        out_shape=jax.ShapeDtypeStruct((db, 1, SELF_WIDTH), F32),
        compiler_params=_params("arbitrary", "arbitrary"),
        name="diff_attn_decode",
    )(pt, q, k_new, v_new, lam_qk, g, *([cache_k] * pages), *([cache_v] * pages))
    return out.reshape(db, SELF_WIDTH)


def _mem_decode_body(qm_ref, mkt_ref, mvt_ref, o_ref):
    lane = lax.broadcasted_iota(jnp.int32, (SUBLANES, MEM_WIDTH), 1)
    row = lax.broadcasted_iota(jnp.int32, (SUBLANES, MEM_WIDTH), 0)
    own = lane // MEM_HEAD_DIM == row
    for r in range(qm_ref.shape[0]):
        qr = jnp.where(own, qm_ref[r:r + 1, :], 0.0)
        s = jnp.dot(qr, mkt_ref[r], preferred_element_type=F32) * MEM_SCALE
        m = jnp.max(s, axis=1, keepdims=True)
        e = jnp.exp(s - m)
        p = e * (1.0 / jnp.sum(e, axis=1, keepdims=True))
        o = lax.dot_general(p, mvt_ref[r], _NT, preferred_element_type=F32)
        o_ref[r:r + 1, :] = jnp.sum(jnp.where(own, o, 0.0), axis=0, keepdims=True)


def _mem_decode(qm, mem_kt, mem_vt, layer):
    db = qm.shape[0]
    rows = MEM_DECODE_ROWS_PER_STEP
    vec = pl.BlockSpec((rows, MEM_WIDTH), lambda b: (b, 0))
    mem = pl.BlockSpec((None, rows, MEM_WIDTH, MEM_LEN), lambda b: (layer, b, 0, 0))
    return pl.pallas_call(
        _mem_decode_body,
        grid=(db // rows,),
        in_specs=[vec, mem, mem],
        out_specs=vec,
        out_shape=jax.ShapeDtypeStruct((db, MEM_WIDTH), F32),
        compiler_params=_params("arbitrary"),
        name="mem_decode",
    )(qm, mem_kt, mem_vt)


def _sgu_body(x_ref, w_ref, lng_ref, lnb_ref, *refs, prompt):
    if prompt:
        ws_ref, sb_ref, mk_ref, mvm_ref, a_ref, m_ref = refs
    else:
        ws_ref, sb_ref, a_ref, m_ref, sv_ref = refs
    sw = SELF_WIDTH

    def project(rows):
        xb = x_ref[rows, :].astype(BF16)
        return (jnp.dot(xb, w_ref[:, 0:sw], preferred_element_type=F32),
                jnp.dot(xb, w_ref[:, sw:2 * sw], preferred_element_type=F32),
                jnp.dot(xb, w_ref[:, 2 * sw:], preferred_element_type=F32))

    if not prompt:
        up, vp, qm = project(slice(None))
        u = _gelu_tanh(up)
        v = _ln(_gelu_tanh(vp), lng_ref[...], lnb_ref[...])
        sv_ref[...] = v
        for g in range(SGU_GROUPS):
            cols = slice(g * CHUNK, (g + 1) * CHUNK)
            z = ws_ref[g, 0:1, 0:1] * v[:, cols] + sb_ref[g:g + 1, 0:1]
            a_ref[:, cols] = u[:, cols] * z
        m_ref[...] = qm
        return

    r = lax.broadcasted_iota(jnp.int32, (CHUNK, CHUNK), 0)
    c = lax.broadcasted_iota(jnp.int32, (CHUNK, CHUNK), 1)
    wms = [jnp.where(r >= c, ws_ref[g], 0.0).astype(BF16) for g in range(SGU_GROUPS)]
    sub = ROW_SUB_TILE
    tiles = [slice(i * sub, (i + 1) * sub) for i in range(x_ref.shape[0] // sub)]
    nxt = project(tiles[0])
    for i, rows in enumerate(tiles):
        up, vp, qm = nxt
        if i + 1 < len(tiles):
            nxt = project(tiles[i + 1])
        u = _gelu_tanh(up)
        vb = _ln(_gelu_tanh(vp), lng_ref[...], lnb_ref[...]).astype(BF16)
        for g in range(SGU_GROUPS):
            cols = slice(g * CHUNK, (g + 1) * CHUNK)
            for n in range(sub // CHUNK):
                rr = slice(n * CHUNK, (n + 1) * CHUNK)
                z = jnp.dot(wms[g], vb[rr, cols], preferred_element_type=F32) + sb_ref[g]
                a_ref[rows.start + n * CHUNK:rows.start + (n + 1) * CHUNK, cols] = (
                    u[rr, cols] * z).astype(BF16)
        m_ref[rows, :] = _mem_attend_shared(qm, mk_ref, mvm_ref).astype(BF16)


def _sgu_prompt(x, w_bf, ln_g, ln_b, sgu_w, sgu_b_full, mk_bf, mvm_bf, layer, seq):
    m = x.shape[0]
    tm = TOKEN_TILE
    per_b = seq // tm
    row = lambda i: (i, 0)
    const2 = lambda i: (0, 0)
    const3 = lambda i: (0, 0, 0)
    return pl.pallas_call(
        functools.partial(_sgu_body, prompt=True),
        grid=(m // tm,),
        in_specs=[
            pl.BlockSpec((tm, D_MODEL), row),
            pl.BlockSpec(w_bf.shape, const2),
            pl.BlockSpec(ln_g.shape, const2),
            pl.BlockSpec(ln_b.shape, const2),
            pl.BlockSpec(sgu_w.shape, const3),
            pl.BlockSpec(sgu_b_full.shape, const3),
            pl.BlockSpec((None, None, MEM_LEN, MEM_WIDTH), lambda i: (layer, i // per_b, 0, 0)),
            pl.BlockSpec((None, None, MEM_HEADS * MEM_LEN, MEM_WIDTH), lambda i: (layer, i // per_b, 0, 0)),
        ],
        out_specs=[pl.BlockSpec((tm, SELF_WIDTH), row), pl.BlockSpec((tm, MEM_WIDTH), row)],
        out_shape=[jax.ShapeDtypeStruct((m, SELF_WIDTH), BF16),
                   jax.ShapeDtypeStruct((m, MEM_WIDTH), BF16)],
        compiler_params=_params("arbitrary"),
        name="sgu_prompt",
    )(x, w_bf, ln_g, ln_b, sgu_w, sgu_b_full, mk_bf, mvm_bf)


def _sgu_sample(x, w_bf, ln_g, ln_b, sgu_w, sgu_b):
    m = x.shape[0]
    c = lambda i: (0, 0)
    return pl.pallas_call(
        functools.partial(_sgu_body, prompt=False),
        grid=(1,),
        in_specs=[
            pl.BlockSpec((m, D_MODEL), c),
            pl.BlockSpec(w_bf.shape, c),
            pl.BlockSpec(ln_g.shape, c),
            pl.BlockSpec(ln_b.shape, c),
            pl.BlockSpec(sgu_w.shape, lambda i: (0, 0, 0)),
            pl.BlockSpec(sgu_b.shape, c),
        ],
        out_specs=[pl.BlockSpec((m, SELF_WIDTH), c), pl.BlockSpec((m, MEM_WIDTH), c),
                   pl.BlockSpec((m, SELF_WIDTH), c)],
        out_shape=[jax.ShapeDtypeStruct((m, SELF_WIDTH), F32),
                   jax.ShapeDtypeStruct((m, MEM_WIDTH), F32),
                   jax.ShapeDtypeStruct((m, SELF_WIDTH), F32)],
        compiler_params=_params("arbitrary"),
        name="sgu_sample",
    )(x, w_bf, ln_g, ln_b, sgu_w, sgu_b)


def _out_ffn_body(x_ref, a_ref, m_ref, wo_ref, g1_ref, b1_ref, wu_ref, wd_ref, g2_ref, b2_ref,
                  o_ref, *, layer):
    ln_row = slice(layer, layer + 1)
    g1, b1, g2, b2 = (r[ln_row, :] for r in (g1_ref, b1_ref, g2_ref, b2_ref))

    def mix_proj(rows):
        d = jnp.dot(a_ref[rows, :].astype(BF16), wo_ref[0:SELF_WIDTH, :],
                    preferred_element_type=F32)
        return d + jnp.dot(m_ref[rows, :].astype(BF16), wo_ref[SELF_WIDTH:, :],
                           preferred_element_type=F32)

    tm = x_ref.shape[0]
    sub = min(ROW_SUB_TILE, tm)
    tiles = [slice(i * sub, (i + 1) * sub) for i in range(tm // sub)]
    nxt = mix_proj(tiles[0])
    for i, rows in enumerate(tiles):
        d = nxt
        if i + 1 < len(tiles):
            nxt = mix_proj(tiles[i + 1])
        y = _ln(ALPHA * x_ref[rows, :] + d, g1, b1)
        yb = y.astype(BF16)
        acc = jnp.zeros_like(y)
        for c in range(D_FF // FF_CHUNK):
            cols = slice(c * FF_CHUNK, (c + 1) * FF_CHUNK)
            h = jnp.maximum(jnp.dot(yb, wu_ref[:, cols], preferred_element_type=F32), 0.0)
            acc = acc + jnp.dot((h * h).astype(BF16), wd_ref[cols, :],
                                preferred_element_type=F32)
        o_ref[rows, :] = _ln(ALPHA * y + acc, g2, b2)


def _out_ffn(x, a, ma, wo_bf, g1, b1, wu_bf, wd_bf, g2, b2, layer, tm, name):
    m = x.shape[0]
    row = lambda i: (i, 0)
    resident = lambda arr: pl.BlockSpec((None,) + arr.shape[1:], lambda i: (layer, 0, 0),
                                        pipeline_mode=pl.Buffered(1))
    ln = lambda arr: pl.BlockSpec(arr.shape, lambda i: (0, 0), pipeline_mode=pl.Buffered(1))
    return pl.pallas_call(
        functools.partial(_out_ffn_body, layer=layer),
        grid=(m // tm,),
        in_specs=[
            pl.BlockSpec((tm, D_MODEL), row),
            pl.BlockSpec((tm, SELF_WIDTH), row),
            pl.BlockSpec((tm, MEM_WIDTH), row),
            resident(wo_bf), ln(g1), ln(b1),
            resident(wu_bf), resident(wd_bf), ln(g2), ln(b2),
        ],
        out_specs=pl.BlockSpec((tm, D_MODEL), row),
        out_shape=jax.ShapeDtypeStruct((m, D_MODEL), F32),
        compiler_params=_params("arbitrary"),
        name=name,
    )(x, a, ma, wo_bf, g1, b1, wu_bf, wd_bf, g2, b2)


def kernel(x_prompt, x_sample, cache_k, cache_v, cache_mem_k, cache_mem_v, page_table, mem_prompt, w_in_attn, lambda_qk, subln_g, w_in_sgu, sgu_ln_g, sgu_ln_b, sgu_w, sgu_b, w_mem_kv, w_out, ln1_g, ln1_b, w_up, w_down, ln2_g, ln2_b):
    nb, seq, _ = x_prompt.shape
    db = x_sample.shape[0]
    row = lambda a: a.reshape(1, -1)

    xp = x_prompt.reshape(nb * seq, D_MODEL)
    xs = x_sample.reshape(db, D_MODEL)
    w_in_attn_bf = w_in_attn.astype(BF16)
    w_in_sgu_bf = w_in_sgu.astype(BF16)
    w_out_bf = w_out.astype(BF16)
    w_up_bf = w_up.astype(BF16)
    w_down_bf = w_down.astype(BF16)
    cache_k_pg = jnp.transpose(cache_k[0], (0, 2, 1, 3))
    cache_v_pg = jnp.transpose(cache_v[0], (0, 2, 1, 3))
    mem_kt = jnp.transpose(cache_mem_k, (0, 1, 3, 4, 2)).reshape(DEPTH, db, MEM_WIDTH, MEM_LEN)
    mem_vt = jnp.transpose(cache_mem_v, (0, 1, 3, 4, 2)).reshape(DEPTH, db, MEM_WIDTH, MEM_LEN)

    mkt_p, mvt_p, mk_bf, mvm_bf = _mem_kv(mem_prompt, w_mem_kv.astype(BF16))

    def out_ffn(x, a, ma, l, tm, name):
        return _out_ffn(x, a, ma, w_out_bf, ln1_g, ln1_b, w_up_bf, w_down_bf, ln2_g, ln2_b,
                        l, tm, name)

    lam0 = _lambda_init(0)
    g0 = row(subln_g[0])
    q_bf, k_pg, v_pg, k_bf, vt_bf, ma_p = _attn_proj_prompt(xp, w_in_attn_bf[0], mk_bf, mvm_bf, 0, seq)
    o_p = _diff_attn_prompt(q_bf, k_bf, vt_bf, lambda_qk[0], g0, lam0)
    xp = out_ffn(xp, o_p.reshape(nb * seq, SELF_WIDTH), ma_p, 0, FFN_TOKEN_TILE, "out_ffn_prompt0")

    q_s, k_s, v_s, qm_s = _attn_proj_sample(xs, w_in_attn_bf[0])
    o_s = _diff_attn_decode(q_s, k_s, v_s, cache_k_pg, cache_v_pg, page_table, lambda_qk[0], g0, lam0)
    ma_s = _mem_decode(qm_s, mem_kt, mem_vt, 0)
    xs = out_ffn(xs, o_s, ma_s, 0, db, "out_ffn_sample0")

    sgu_b_full = jnp.broadcast_to(sgu_b[0][:, :, None], (SGU_GROUPS, CHUNK, CHUNK))
    a_p, ma_p = _sgu_prompt(xp, w_in_sgu_bf[0], row(sgu_ln_g[0]), row(sgu_ln_b[0]), sgu_w[0],
                            sgu_b_full, mk_bf, mvm_bf, 1, seq)
    xp = out_ffn(xp, a_p, ma_p, 1, FFN_TOKEN_TILE, "out_ffn_prompt1")

    a_s, qm_s, sv_s = _sgu_sample(xs, w_in_sgu_bf[0], row(sgu_ln_g[0]), row(sgu_ln_b[0]),
                                  sgu_w[0], sgu_b[0])
    ma_s = _mem_decode(qm_s, mem_kt, mem_vt, 1)
    xs = out_ffn(xs, a_s, ma_s, 1, db, "out_ffn_sample1")

    n_pg = seq // PAGE_SIZE
    paged = lambda a: jnp.transpose(
        a.reshape(1, nb, n_pg, DIFF_HEADS, PAGE_SIZE, HEAD_WIDTH), (0, 1, 2, 4, 3, 5))
    mem_out = lambda a: jnp.transpose(
        a.reshape(DEPTH, nb, MEM_HEADS, MEM_HEAD_DIM, MEM_LEN), (0, 1, 4, 2, 3))
    dec_shape = (1, db, 1, DIFF_HEADS, HEAD_WIDTH)
    return (xp.reshape(nb, seq, D_MODEL), xs.reshape(db, 1, D_MODEL),
            paged(k_pg), paged(v_pg), mem_out(mkt_p), mem_out(mvt_p),
            k_s.reshape(dec_shape), v_s.reshape(dec_shape), sv_s.reshape(dec_shape))
```

```python
import functools
import math

import jax
import jax.numpy as jnp
from jax import lax
from jax.experimental import pallas as pl
from jax.experimental.pallas import tpu as pltpu

F32 = jnp.float32
BF16 = jnp.bfloat16

D_MODEL = 1024
DEPTH = 2
PAGE_SIZE = 128
MEM_LEN = 256
MEM_HEADS = 4
MEM_HEAD_DIM = 64
MEM_WIDTH = MEM_HEADS * MEM_HEAD_DIM
SELF_WIDTH = D_MODEL - MEM_WIDTH
ATTN_IN = 3 * SELF_WIDTH + MEM_WIDTH
DIFF_HEAD_DIM = 64
DIFF_HEADS = SELF_WIDTH // (2 * DIFF_HEAD_DIM)
HEAD_WIDTH = 2 * DIFF_HEAD_DIM
CHUNK = 128
SGU_GROUPS = 6
D_FF = 4 * D_MODEL
ALPHA = (2 * DEPTH) ** 0.25
LN_EPS = 1e-5
NEG_INF = -1e30
QK_SCALE = DIFF_HEAD_DIM ** -0.5
QK_SCALE_LOG2 = QK_SCALE * math.log2(math.e)
MEM_SCALE = MEM_HEAD_DIM ** -0.5

LANES = 128
SUBLANES = 8
BF16_SUBLANES = 16
VT_ROWS = HEAD_WIDTH + BF16_SUBLANES
VMEM_LIMIT_BYTES = 56 * 1024 * 1024

TOKEN_TILE = 1024
FFN_TOKEN_TILE = 1024
ROW_SUB_TILE = 256
ATTN_BLOCK = 1024
FF_CHUNK = 1024
DECODE_PAGES_PER_STEP = 16
DECODE_MAPS = 16
MEM_DECODE_ROWS_PER_STEP = 8

_NT = (((1,), (1,)), ((), ()))


def _lambda_init(layer):
    return 0.8 - 0.6 * math.exp(-0.3 * layer)


def _params(*sem):
    return pltpu.CompilerParams(dimension_semantics=sem, vmem_limit_bytes=VMEM_LIMIT_BYTES)


def _ln(x, g, b):
    mu = jnp.mean(x, axis=-1, keepdims=True)
    xc = x - mu
    var = jnp.mean(xc * xc, axis=-1, keepdims=True)
    return xc * lax.rsqrt(var + LN_EPS) * g + b


def _gelu_tanh(x):
    c = math.sqrt(2.0 / math.pi)
    return x * (0.5 * (1.0 + jnp.tanh(c * (x + 0.044715 * (x * x * x)))))


def _diff_lambda(lam_ref, lam0):
    l = lam_ref[...]
    a = jnp.sum(l[0:1, :] * l[1:2, :], axis=1, keepdims=True)
    b = jnp.sum(l[2:3, :] * l[3:4, :], axis=1, keepdims=True)
    return jnp.exp(a) - jnp.exp(b) + lam0


def _mem_kv_body(mem_ref, w_ref, mk_ref, mv_ref, mkb_ref, mvm_ref):
    kv = jnp.dot(mem_ref[...].astype(BF16), w_ref[...], preferred_element_type=F32)
    mk = kv[:, :MEM_WIDTH]
    mv = kv[:, MEM_WIDTH:]
    mk_ref[...] = mk.T
    mv_ref[...] = mv.T
    mkb_ref[...] = mk.astype(BF16)
    lane = lax.broadcasted_iota(jnp.int32, mv.shape, 1)
    for h in range(MEM_HEADS):
        mvm_ref[h * MEM_LEN:(h + 1) * MEM_LEN, :] = jnp.where(
            lane // MEM_HEAD_DIM == h, mv, 0.0).astype(BF16)


def _mem_kv(mem_prompt, w_mem_kv_bf):
    nb = mem_prompt.shape[0]
    per_lb = lambda l, b: (l, b, 0, 0)
    return pl.pallas_call(
        _mem_kv_body,
        grid=(DEPTH, nb),
        in_specs=[
            pl.BlockSpec((None, MEM_LEN, D_MODEL), lambda l, b: (b, 0, 0)),
            pl.BlockSpec((None, D_MODEL, 2 * MEM_WIDTH), lambda l, b: (l, 0, 0)),
        ],
        out_specs=[
            pl.BlockSpec((None, None, MEM_WIDTH, MEM_LEN), per_lb),
            pl.BlockSpec((None, None, MEM_WIDTH, MEM_LEN), per_lb),
            pl.BlockSpec((None, None, MEM_LEN, MEM_WIDTH), per_lb),
            pl.BlockSpec((None, None, MEM_HEADS * MEM_LEN, MEM_WIDTH), per_lb),
        ],
        out_shape=[
            jax.ShapeDtypeStruct((DEPTH, nb, MEM_WIDTH, MEM_LEN), F32),
            jax.ShapeDtypeStruct((DEPTH, nb, MEM_WIDTH, MEM_LEN), F32),
            jax.ShapeDtypeStruct((DEPTH, nb, MEM_LEN, MEM_WIDTH), BF16),
            jax.ShapeDtypeStruct((DEPTH, nb, MEM_HEADS * MEM_LEN, MEM_WIDTH), BF16),
        ],
        compiler_params=_params("arbitrary", "arbitrary"),
        name="mem_kv",
    )(mem_prompt, w_mem_kv_bf)


def _mem_attend_shared(qm, mk_ref, mvm_ref):
    lane = lax.broadcasted_iota(jnp.int32, qm.shape, 1)
    mk = mk_ref[...]
    ps = []
    for h in range(MEM_HEADS):
        qh = jnp.where(lane // MEM_HEAD_DIM == h, qm, 0.0).astype(BF16)
        s = lax.dot_general(qh, mk, _NT, preferred_element_type=F32) * MEM_SCALE
        m = jnp.max(s, axis=1, keepdims=True)
        e = jnp.exp(s - m)
        l = jnp.sum(e, axis=1, keepdims=True)
        ps.append((e * (1.0 / l)).astype(BF16))
    p = jnp.concatenate(ps, axis=1)
    return jnp.dot(p, mvm_ref[...], preferred_element_type=F32)


def _attn_proj_body(x_ref, w_ref, *refs, with_mem, n_cast=0):
    if with_mem:
        mk_ref, mvm_ref = refs[:2]
        cast_in = refs[2:2 + n_cast]
        q_ref, kf_ref, vf_ref, kb_ref, vt_ref, m_ref = refs[2 + n_cast:8 + n_cast]
        cast_out = refs[8 + n_cast:]
        for src, dst in zip(cast_in, cast_out):
            dst[...] = src[...].astype(BF16)
    else:
        q_ref, kf_ref, vf_ref, m_ref = refs
    sw = SELF_WIDTH

    def project(rows):
        xb = x_ref[rows, :].astype(BF16)
        return tuple(jnp.dot(xb, w_ref[:, lo:hi], preferred_element_type=F32)
                     for lo, hi in ((0, sw), (sw, 2 * sw), (2 * sw, 3 * sw), (3 * sw, ATTN_IN)))

    if not with_mem:
        q, k, v, qm = project(slice(None))
        kf_ref[...] = k
        vf_ref[...] = v
        q_ref[...] = q
        m_ref[...] = qm
        return

    sub = ROW_SUB_TILE
    tiles = [slice(i * sub, (i + 1) * sub) for i in range(x_ref.shape[0] // sub)]
    nxt = project(tiles[0])
    for i, rows in enumerate(tiles):
        q, k, v, qm = nxt
        if i + 1 < len(tiles):
            nxt = project(tiles[i + 1])
        for h in range(DIFF_HEADS):
            cols = slice(h * HEAD_WIDTH, (h + 1) * HEAD_WIDTH)
            for p in range(sub // PAGE_SIZE):
                rr = slice(p * PAGE_SIZE, (p + 1) * PAGE_SIZE)
                page = rows.start // PAGE_SIZE + p
                kf_ref[page, h] = k[rr, cols]
                vf_ref[page, h] = v[rr, cols]
            q_ref[h, rows, :] = (q[:, cols] * QK_SCALE_LOG2).astype(BF16)
            kb_ref[h, rows, :] = k[:, cols].astype(BF16)
            vt_ref[h, 0:HEAD_WIDTH, rows] = v[:, cols].T.astype(BF16)
            vt_ref[h, HEAD_WIDTH:, rows] = jnp.ones((BF16_SUBLANES, sub), BF16)
        m_ref[rows, :] = _mem_attend_shared(qm, mk_ref, mvm_ref).astype(BF16)


def _attn_proj_prompt(x, w_bf, mk_bf, mvm_bf, layer, seq, cast_weights):
    m = x.shape[0]
    tm = TOKEN_TILE
    per_b = seq // tm
    pages = tm // PAGE_SIZE
    row = lambda i: (i, 0)
    nb = m // seq
    per_blk = ATTN_BLOCK // tm
    paged = pl.BlockSpec((pages, DIFF_HEADS, PAGE_SIZE, HEAD_WIDTH), lambda i: (i, 0, 0, 0))
    paged_shape = jax.ShapeDtypeStruct((m // PAGE_SIZE, DIFF_HEADS, PAGE_SIZE, HEAD_WIDTH), F32)
    heads = pl.BlockSpec((None, DIFF_HEADS, tm, HEAD_WIDTH), lambda i: (i // per_b, 0, i % per_b, 0))
    heads_shape = jax.ShapeDtypeStruct((nb, DIFF_HEADS, seq, HEAD_WIDTH), BF16)
    steps = m // tm
    slab = lambda a: pl.BlockSpec((a.shape[0] // steps, a.shape[1]), row)
    for a in cast_weights:
        assert a.shape[0] % (steps * BF16_SUBLANES) == 0
    return pl.pallas_call(
        functools.partial(_attn_proj_body, with_mem=True, n_cast=len(cast_weights)),
        grid=(steps,),
        in_specs=[
            pl.BlockSpec((tm, D_MODEL), row),
            pl.BlockSpec(w_bf.shape, lambda i: (0, 0), pipeline_mode=pl.Buffered(1)),
            pl.BlockSpec((None, None, MEM_LEN, MEM_WIDTH), lambda i: (layer, i // per_b, 0, 0)),
            pl.BlockSpec((None, None, MEM_HEADS * MEM_LEN, MEM_WIDTH), lambda i: (layer, i // per_b, 0, 0)),
        ] + [slab(a) for a in cast_weights],
        out_specs=[
            heads,
            paged,
            paged,
            heads,
            pl.BlockSpec((None, DIFF_HEADS, None, VT_ROWS, tm),
                         lambda i: (i // per_b, 0, (i % per_b) // per_blk, 0, (i % per_b) % per_blk)),
            pl.BlockSpec((tm, MEM_WIDTH), row),
        ] + [slab(a) for a in cast_weights],
        out_shape=[
            heads_shape,
            paged_shape,
            paged_shape,
            heads_shape,
            jax.ShapeDtypeStruct((nb, DIFF_HEADS, seq // ATTN_BLOCK, VT_ROWS, ATTN_BLOCK), BF16),
            jax.ShapeDtypeStruct((m, MEM_WIDTH), BF16),
        ] + [jax.ShapeDtypeStruct(a.shape, BF16) for a in cast_weights],
        compiler_params=_params("arbitrary"),
        name="attn_proj_prompt",
    )(x, w_bf, mk_bf, mvm_bf, *cast_weights)


def _attn_proj_sample(x, w_bf):
    m = x.shape[0]
    row = lambda i: (0, 0)
    return pl.pallas_call(
        functools.partial(_attn_proj_body, with_mem=False),
        grid=(1,),
        in_specs=[pl.BlockSpec((m, D_MODEL), row), pl.BlockSpec(w_bf.shape, row)],
        out_specs=[
            pl.BlockSpec((m, SELF_WIDTH), row),
            pl.BlockSpec((m, SELF_WIDTH), row),
            pl.BlockSpec((m, SELF_WIDTH), row),
            pl.BlockSpec((m, MEM_WIDTH), row),
        ],
        out_shape=[
            jax.ShapeDtypeStruct((m, SELF_WIDTH), F32),
            jax.ShapeDtypeStruct((m, SELF_WIDTH), F32),
            jax.ShapeDtypeStruct((m, SELF_WIDTH), F32),
            jax.ShapeDtypeStruct((m, MEM_WIDTH), F32),
        ],
        compiler_params=_params("arbitrary"),
        name="attn_proj_sample",
    )(x, w_bf)


def _diff_attn_body(q_ref, k_ref, vt_ref, lam_ref, g_ref, o_ref,
                    qs_ref, m_ref, l_ref, acc_ref, sbuf_ref, *, blk, lam0):
    qi = pl.program_id(2)
    sq = blk // 2
    chains = [(1, 0), (1, 1), (0, 0), (0, 1)]

    def _init():
        q = q_ref[...]
        lane = lax.broadcasted_iota(jnp.int32, q.shape, 1)
        zero = jnp.zeros_like(q)
        qs_ref[0] = jnp.where(lane < DIFF_HEAD_DIM, q, zero)
        qs_ref[1] = jnp.where(lane >= DIFF_HEAD_DIM, q, zero)
        m_ref[...] = jnp.full(m_ref.shape, NEG_INF, F32)
        l_ref[...] = jnp.zeros(l_ref.shape, F32)
        acc_ref[...] = jnp.zeros(acc_ref.shape, F32)

    def cols(qb):
        return slice(qb * sq, (qb + 1) * sq)

    def scores(j, qb, c, nk=blk):
        k = k_ref[pl.ds(pl.multiple_of(j * blk, blk), nk), :]
        return lax.dot_general(k, qs_ref[c, cols(qb), :], _NT, preferred_element_type=F32)

    def causal(s, qb):
        key = lax.broadcasted_iota(jnp.int32, s.shape, 0)
        qry = qb * sq + lax.broadcasted_iota(jnp.int32, s.shape, 1)
        return jnp.where(key <= qry, s, NEG_INF)

    def chain_max(s, qb, c):
        m_prev = m_ref[c, 0:1, cols(qb)]
        m_new = jnp.maximum(m_prev, jnp.max(s, axis=0, keepdims=True))
        return m_new, jnp.exp2(m_prev - m_new)

    def reduce_chain(s, stats, vt, qb, c):
        m_new, alpha = stats
        p = jnp.exp2(s - m_new).astype(BF16)
        pv = jnp.dot(vt, p, preferred_element_type=F32)
        l_new = alpha * l_ref[c, 0:1, cols(qb)] + pv[HEAD_WIDTH:HEAD_WIDTH + 1, :]
        acc_ref[c, :, cols(qb)] = alpha * acc_ref[c, :, cols(qb)] + pv[0:HEAD_WIDTH, :]
        m_ref[c, :, cols(qb)] = jnp.broadcast_to(m_new, (SUBLANES, sq))
        l_ref[c, :, cols(qb)] = jnp.broadcast_to(l_new, (SUBLANES, sq))

    _init()
    sbuf_ref[0] = scores(0, *chains[0])
    sbuf_ref[1] = scores(0, *chains[1])

    def _plain_block(j):
        vt = vt_ref[j]
        s0, s1 = sbuf_ref[0], sbuf_ref[1]
        s2 = scores(j, *chains[2])
        st0 = chain_max(s0, *chains[0])
        st1 = chain_max(s1, *chains[1])
        reduce_chain(s0, st0, vt, *chains[0])
        s3 = scores(j, *chains[3])
        st2 = chain_max(s2, *chains[2])
        reduce_chain(s1, st1, vt, *chains[1])
        sbuf_ref[0] = scores(j + 1, *chains[0])
        st3 = chain_max(s3, *chains[3])
        reduce_chain(s2, st2, vt, *chains[2])
        sbuf_ref[1] = scores(j + 1, *chains[1])
        reduce_chain(s3, st3, vt, *chains[3])

    def _two_blocks(i, carry):
        _plain_block(2 * i)
        _plain_block(2 * i + 1)
        return carry

    def _last_odd_block(i, carry):
        _plain_block(qi - 1)
        return carry

    lax.fori_loop(0, lax.shift_right_logical(qi, 1), _two_blocks, 0)
    lax.fori_loop(0, qi & 1, _last_odd_block, 0)

    vt = vt_ref[qi]
    s0 = causal(sbuf_ref[0], chains[0][0])
    s1 = causal(sbuf_ref[1], chains[1][0])
    s2 = causal(scores(qi, *chains[2], nk=sq), chains[2][0])
    st0 = chain_max(s0, *chains[0])
    st1 = chain_max(s1, *chains[1])
    reduce_chain(s0, st0, vt, *chains[0])
    s3 = causal(scores(qi, *chains[3], nk=sq), chains[3][0])
    st2 = chain_max(s2, *chains[2])
    reduce_chain(s1, st1, vt, *chains[1])
    st3 = chain_max(s3, *chains[3])
    reduce_chain(s2, st2, vt[:, 0:sq], *chains[2])
    reduce_chain(s3, st3, vt[:, 0:sq], *chains[3])

    lam = _diff_lambda(lam_ref, lam0)
    inv0 = 1.0 / l_ref[0, 0:1, :]
    inv1 = 1.0 / l_ref[1, 0:1, :]
    ot = acc_ref[0] * inv0 - lam * (acc_ref[1] * inv1)
    ms = jnp.mean(ot * ot, axis=0, keepdims=True)
    yt = ot * lax.rsqrt(ms + LN_EPS)
    o_ref[...] = (yt.T * (g_ref[...] * (1.0 - lam0))).astype(o_ref.dtype)


def _diff_attn_prompt(q_bf, k_bf, vt_bf, lam_qk, g, lam0):
    nb, _, seq, _ = q_bf.shape
    blk = ATTN_BLOCK
    n_blk = seq // blk
    per_head = lambda b, h, i: (b, h, 0, 0)
    return pl.pallas_call(
        functools.partial(_diff_attn_body, blk=blk, lam0=lam0),
        grid=(nb, DIFF_HEADS, n_blk),
        in_specs=[
            pl.BlockSpec((None, None, blk, HEAD_WIDTH), lambda b, h, i: (b, h, i, 0)),
            pl.BlockSpec((None, None, seq, HEAD_WIDTH), per_head),
            pl.BlockSpec((None, None, n_blk, VT_ROWS, blk), lambda b, h, i: (b, h, 0, 0, 0)),
            pl.BlockSpec(lam_qk.shape, lambda b, h, i: (0, 0)),
            pl.BlockSpec(g.shape, lambda b, h, i: (0, 0)),
        ],
        out_specs=pl.BlockSpec((None, blk, HEAD_WIDTH), lambda b, h, i: (b, i, h)),
        out_shape=jax.ShapeDtypeStruct((nb, seq, SELF_WIDTH), BF16),
        scratch_shapes=[
            pltpu.VMEM((2, blk, HEAD_WIDTH), BF16),
            pltpu.VMEM((2, SUBLANES, blk), F32),
            pltpu.VMEM((2, SUBLANES, blk), F32),
            pltpu.VMEM((2, HEAD_WIDTH, blk), F32),
            pltpu.VMEM((2, blk, blk // 2), F32),
        ],
        compiler_params=_params("arbitrary", "arbitrary", "arbitrary"),
        name="diff_attn_prompt",
    )(q_bf, k_bf, vt_bf, lam_qk, g)


def _diff_attn_decode_body(pt_ref, q_ref, kn_ref, vn_ref, qm_ref, mkt_ref, mvt_ref, lam_ref, g_ref,
                           *refs, pages, lam0):
    k_refs = refs[:pages]
    v_refs = refs[pages:2 * pages]
    o_ref, om_ref, wq_ref, m_ref, l_ref, acc_ref = refs[2 * pages:]
    j = pl.program_id(1)
    last = j == pl.num_programs(1) - 1
    maps = DECODE_MAPS
    _TN = (((0,), (0,)), ((), ()))
    eye = (lax.broadcasted_iota(jnp.int32, (maps, maps), 0)
           == lax.broadcasted_iota(jnp.int32, (maps, maps), 1))

    def column(x_row):
        return jnp.sum(jnp.where(eye, jnp.broadcast_to(x_row, (maps, maps)), 0.0),
                       axis=1, keepdims=True)

    def own_row(ref):
        b = pl.program_id(0)
        grp = ref[pl.ds(pl.multiple_of((b // SUBLANES) * SUBLANES, SUBLANES), SUBLANES), :]
        sub = lax.broadcasted_iota(jnp.int32, grp.shape, 0)
        return jnp.sum(jnp.where(sub == b % SUBLANES, grp, 0.0), axis=0, keepdims=True)

    def page(refs_, p):
        return jnp.concatenate([refs_[p][h] for h in range(DIFF_HEADS)], axis=1)

    @pl.when(j == 0)
    def _init():
        lane = lax.broadcasted_iota(jnp.int32, wq_ref.shape, 1)
        r = lax.broadcasted_iota(jnp.int32, wq_ref.shape, 0)
        wq_ref[...] = jnp.where(lane // DIFF_HEAD_DIM == r, own_row(q_ref) * QK_SCALE, 0.0)
        m_ref[...] = jnp.full(m_ref.shape, NEG_INF, F32)
        l_ref[...] = jnp.zeros(l_ref.shape, F32)
        acc_ref[...] = jnp.zeros(acc_ref.shape, F32)
        mlane = lax.broadcasted_iota(jnp.int32, (SUBLANES, MEM_WIDTH), 1)
        mrow = lax.broadcasted_iota(jnp.int32, (SUBLANES, MEM_WIDTH), 0)
        mine = mlane // MEM_HEAD_DIM == mrow
        sm = jnp.dot(jnp.where(mine, own_row(qm_ref), 0.0), mkt_ref[...],
                     preferred_element_type=F32) * MEM_SCALE
        em = jnp.exp(sm - jnp.max(sm, axis=1, keepdims=True))
        pm = em * (1.0 / jnp.sum(em, axis=1, keepdims=True))
        om = lax.dot_general(pm, mvt_ref[...], _NT, preferred_element_type=F32)
        om_ref[...] = jnp.sum(jnp.where(mine, om, 0.0), axis=0, keepdims=True)

    wq = wq_ref[...]
    s = jnp.concatenate(
        [lax.dot_general(page(k_refs, p), wq, _NT, preferred_element_type=F32)
         for p in range(pages)], axis=0)
    m_prev = m_ref[0:1, :]
    m_new = jnp.maximum(m_prev, jnp.max(s, axis=0, keepdims=True))
    alpha = jnp.exp(m_prev - m_new)
    e = jnp.exp(s - m_new)
    l_new = alpha * l_ref[0:1, :] + jnp.sum(e, axis=0, keepdims=True)
    acc = column(alpha) * acc_ref[...]
    for p in range(pages):
        acc = acc + lax.dot_general(e[p * PAGE_SIZE:(p + 1) * PAGE_SIZE, :], page(v_refs, p), _TN,
                                    preferred_element_type=F32)
    acc_ref[...] = acc
    m_ref[...] = jnp.broadcast_to(m_new, m_ref.shape)
    l_ref[...] = jnp.broadcast_to(l_new, l_ref.shape)

    @pl.when(last)
    def _finish():
        lam = _diff_lambda(lam_ref, lam0)
        k_new = jnp.broadcast_to(own_row(kn_ref), (SUBLANES, SELF_WIDTH))
        s_new = lax.dot_general(k_new, wq, _NT, preferred_element_type=F32)[0:1, :]
        m_fin = jnp.maximum(m_new, s_new)
        a_fin = jnp.exp(m_new - m_fin)
        p_new = jnp.exp(s_new - m_fin)
        l_fin = a_fin * l_new + p_new
        r_row = lax.broadcasted_iota(jnp.int32, (1, maps), 1)
        coef = jnp.where(r_row % 2 == 0, 1.0, -lam) / l_fin
        coef = jnp.where(r_row < 2 * DIFF_HEADS, coef, 0.0)
        acc_fin = column(a_fin) * acc + column(p_new) * own_row(vn_ref)
        lane = lax.broadcasted_iota(jnp.int32, acc_fin.shape, 1)
        r = lax.broadcasted_iota(jnp.int32, acc_fin.shape, 0)
        own = lane // HEAD_WIDTH == r // 2
        o = jnp.sum(jnp.where(own, acc_fin * column(coef), 0.0), axis=0, keepdims=True)
        scale = g_ref[...] * (1.0 - lam0)
        for h in range(DIFF_HEADS):
            cols = slice(h * HEAD_WIDTH, (h + 1) * HEAD_WIDTH)
            oh = o[:, cols]
            ms = jnp.mean(oh * oh, axis=1, keepdims=True)
            o_ref[:, cols] = oh * lax.rsqrt(ms + LN_EPS) * scale


def _diff_attn_decode(q, k_new, v_new, qm, mem_kt, mem_vt, layer, cache_k, cache_v, page_table,
                      lam_qk, g, lam0):
    db = q.shape[0]
    n_pages = page_table.shape[1]
    pages = DECODE_PAGES_PER_STEP
    assert n_pages % pages == 0
    pt = page_table.reshape(-1)
    vec = pl.BlockSpec((db, SELF_WIDTH), lambda b, j, pt: (0, 0))
    out_row = pl.BlockSpec((None, 1, SELF_WIDTH), lambda b, j, pt: (b, 0, 0))

    def page_spec(p):
        return pl.BlockSpec(
            (None, DIFF_HEADS, PAGE_SIZE, HEAD_WIDTH),
            lambda b, j, pt: (pt[b * n_pages + j * pages + p], 0, 0, 0))

    mem = pl.BlockSpec((None, None, MEM_WIDTH, MEM_LEN), lambda b, j, pt: (layer, b, 0, 0))
    stat = pltpu.VMEM((SUBLANES, DECODE_MAPS), F32)
    grid_spec = pltpu.PrefetchScalarGridSpec(
        num_scalar_prefetch=1,
        grid=(db, n_pages // pages),
        in_specs=[vec, vec, vec,
                  pl.BlockSpec((db, MEM_WIDTH), lambda b, j, pt: (0, 0)), mem, mem,
                  pl.BlockSpec(lam_qk.shape, lambda b, j, pt: (0, 0)),
                  pl.BlockSpec(g.shape, lambda b, j, pt: (0, 0))]
                 + [page_spec(p) for p in range(pages)]
                 + [page_spec(p) for p in range(pages)],
        out_specs=[out_row, pl.BlockSpec((None, 1, MEM_WIDTH), lambda b, j, pt: (b, 0, 0))],
        scratch_shapes=[pltpu.VMEM((DECODE_MAPS, SELF_WIDTH), F32), stat, stat,
                        pltpu.VMEM((DECODE_MAPS, SELF_WIDTH), F32)],
    )
    out, out_mem = pl.pallas_call(
        functools.partial(_diff_attn_decode_body, pages=pages, lam0=lam0),
        grid_spec=grid_spec,
        out_shape=[jax.ShapeDtypeStruct((db, 1, SELF_WIDTH), F32),
                   jax.ShapeDtypeStruct((db, 1, MEM_WIDTH), F32)],
        compiler_params=_params("arbitrary", "arbitrary"),
        name="diff_attn_decode",
    )(pt, q, k_new, v_new, qm, mem_kt, mem_vt, lam_qk, g,
      *([cache_k] * pages), *([cache_v] * pages))
    return out.reshape(db, SELF_WIDTH), out_mem.reshape(db, MEM_WIDTH)


def _mem_decode_body(qm_ref, mkt_ref, mvt_ref, o_ref):
    lane = lax.broadcasted_iota(jnp.int32, (SUBLANES, MEM_WIDTH), 1)
    row = lax.broadcasted_iota(jnp.int32, (SUBLANES, MEM_WIDTH), 0)
    own = lane // MEM_HEAD_DIM == row
    for r in range(qm_ref.shape[0]):
        qr = jnp.where(own, qm_ref[r:r + 1, :], 0.0)
        s = jnp.dot(qr, mkt_ref[r], preferred_element_type=F32) * MEM_SCALE
        m = jnp.max(s, axis=1, keepdims=True)
        e = jnp.exp(s - m)
        p = e * (1.0 / jnp.sum(e, axis=1, keepdims=True))
        o = lax.dot_general(p, mvt_ref[r], _NT, preferred_element_type=F32)
        o_ref[r:r + 1, :] = jnp.sum(jnp.where(own, o, 0.0), axis=0, keepdims=True)


def _mem_decode(qm, mem_kt, mem_vt, layer):
    db = qm.shape[0]
    rows = MEM_DECODE_ROWS_PER_STEP
    vec = pl.BlockSpec((rows, MEM_WIDTH), lambda b: (b, 0))
    mem = pl.BlockSpec((None, rows, MEM_WIDTH, MEM_LEN), lambda b: (layer, b, 0, 0))
    return pl.pallas_call(
        _mem_decode_body,
        grid=(db // rows,),
        in_specs=[vec, mem, mem],
        out_specs=vec,
        out_shape=jax.ShapeDtypeStruct((db, MEM_WIDTH), F32),
        compiler_params=_params("arbitrary"),
        name="mem_decode",
    )(qm, mem_kt, mem_vt)


def _sgu_body(x_ref, w_ref, lng_ref, lnb_ref, *refs, prompt):
    if prompt:
        ws_ref, sb_ref, mk_ref, mvm_ref, a_ref, m_ref = refs
    else:
        ws_ref, sb_ref, a_ref, m_ref, sv_ref = refs
    sw = SELF_WIDTH

    def project(rows):
        xb = x_ref[rows, :].astype(BF16)
        return (jnp.dot(xb, w_ref[:, 0:sw], preferred_element_type=F32),
                jnp.dot(xb, w_ref[:, sw:2 * sw], preferred_element_type=F32),
                jnp.dot(xb, w_ref[:, 2 * sw:], preferred_element_type=F32))

    if not prompt:
        up, vp, qm = project(slice(None))
        u = _gelu_tanh(up)
        v = _ln(_gelu_tanh(vp), lng_ref[...], lnb_ref[...])
        sv_ref[...] = v
        for g in range(SGU_GROUPS):
            cols = slice(g * CHUNK, (g + 1) * CHUNK)
            z = ws_ref[g, 0:1, 0:1] * v[:, cols] + sb_ref[g:g + 1, 0:1]
            a_ref[:, cols] = u[:, cols] * z
        m_ref[...] = qm
        return

    r = lax.broadcasted_iota(jnp.int32, (CHUNK, CHUNK), 0)
    c = lax.broadcasted_iota(jnp.int32, (CHUNK, CHUNK), 1)
    wms = [jnp.where(r >= c, ws_ref[g], 0.0).astype(BF16) for g in range(SGU_GROUPS)]
    sub = ROW_SUB_TILE
    tiles = [slice(i * sub, (i + 1) * sub) for i in range(x_ref.shape[0] // sub)]
    nxt = project(tiles[0])
    for i, rows in enumerate(tiles):
        up, vp, qm = nxt
        if i + 1 < len(tiles):
            nxt = project(tiles[i + 1])
        u = _gelu_tanh(up)
        vb = _ln(_gelu_tanh(vp), lng_ref[...], lnb_ref[...]).astype(BF16)
        for g in range(SGU_GROUPS):
            cols = slice(g * CHUNK, (g + 1) * CHUNK)
            for n in range(sub // CHUNK):
                rr = slice(n * CHUNK, (n + 1) * CHUNK)
                z = jnp.dot(wms[g], vb[rr, cols], preferred_element_type=F32) + sb_ref[g]
                a_ref[rows.start + n * CHUNK:rows.start + (n + 1) * CHUNK, cols] = (
                    u[rr, cols] * z).astype(BF16)
        m_ref[rows, :] = _mem_attend_shared(qm, mk_ref, mvm_ref).astype(BF16)


def _sgu_prompt(x, w_bf, ln_g, ln_b, sgu_w, sgu_b_full, mk_bf, mvm_bf, layer, seq):
    m = x.shape[0]
    tm = TOKEN_TILE
    per_b = seq // tm
    row = lambda i: (i, 0)
    const2 = lambda i: (0, 0)
    const3 = lambda i: (0, 0, 0)
    return pl.pallas_call(
        functools.partial(_sgu_body, prompt=True),
        grid=(m // tm,),
        in_specs=[
            pl.BlockSpec((tm, D_MODEL), row),
            pl.BlockSpec(w_bf.shape, const2),
            pl.BlockSpec(ln_g.shape, const2),
            pl.BlockSpec(ln_b.shape, const2),
            pl.BlockSpec(sgu_w.shape, const3),
            pl.BlockSpec(sgu_b_full.shape, const3),
            pl.BlockSpec((None, None, MEM_LEN, MEM_WIDTH), lambda i: (layer, i // per_b, 0, 0)),
            pl.BlockSpec((None, None, MEM_HEADS * MEM_LEN, MEM_WIDTH), lambda i: (layer, i // per_b, 0, 0)),
        ],
        out_specs=[pl.BlockSpec((tm, SELF_WIDTH), row), pl.BlockSpec((tm, MEM_WIDTH), row)],
        out_shape=[jax.ShapeDtypeStruct((m, SELF_WIDTH), BF16),
                   jax.ShapeDtypeStruct((m, MEM_WIDTH), BF16)],
        compiler_params=_params("arbitrary"),
        name="sgu_prompt",
    )(x, w_bf, ln_g, ln_b, sgu_w, sgu_b_full, mk_bf, mvm_bf)


def _sgu_sample(x, w_bf, ln_g, ln_b, sgu_w, sgu_b):
    m = x.shape[0]
    c = lambda i: (0, 0)
    return pl.pallas_call(
        functools.partial(_sgu_body, prompt=False),
        grid=(1,),
        in_specs=[
            pl.BlockSpec((m, D_MODEL), c),
            pl.BlockSpec(w_bf.shape, c),
            pl.BlockSpec(ln_g.shape, c),
            pl.BlockSpec(ln_b.shape, c),
            pl.BlockSpec(sgu_w.shape, lambda i: (0, 0, 0)),
            pl.BlockSpec(sgu_b.shape, c),
        ],
        out_specs=[pl.BlockSpec((m, SELF_WIDTH), c), pl.BlockSpec((m, MEM_WIDTH), c),
                   pl.BlockSpec((m, SELF_WIDTH), c)],
        out_shape=[jax.ShapeDtypeStruct((m, SELF_WIDTH), F32),
                   jax.ShapeDtypeStruct((m, MEM_WIDTH), F32),
                   jax.ShapeDtypeStruct((m, SELF_WIDTH), F32)],
        compiler_params=_params("arbitrary"),
        name="sgu_sample",
    )(x, w_bf, ln_g, ln_b, sgu_w, sgu_b)


def _out_ffn_body(x_ref, a_ref, m_ref, wo_ref, g1_ref, b1_ref, wu_ref, wd_ref, g2_ref, b2_ref,
                  o_ref, *, layer):
    ln_row = slice(layer, layer + 1)
    g1, b1, g2, b2 = (r[ln_row, :] for r in (g1_ref, b1_ref, g2_ref, b2_ref))

    def mix_proj(rows):
        d = jnp.dot(a_ref[rows, :].astype(BF16), wo_ref[0:SELF_WIDTH, :],
                    preferred_element_type=F32)
        return d + jnp.dot(m_ref[rows, :].astype(BF16), wo_ref[SELF_WIDTH:, :],
                           preferred_element_type=F32)

    tm = x_ref.shape[0]
    sub = min(ROW_SUB_TILE, tm)
    tiles = [slice(i * sub, (i + 1) * sub) for i in range(tm // sub)]
    nxt = mix_proj(tiles[0])
    for i, rows in enumerate(tiles):
        d = nxt
        if i + 1 < len(tiles):
            nxt = mix_proj(tiles[i + 1])
        y = _ln(ALPHA * x_ref[rows, :] + d, g1, b1)
        yb = y.astype(BF16)
        acc = jnp.zeros_like(y)
        for c in range(D_FF // FF_CHUNK):
            cols = slice(c * FF_CHUNK, (c + 1) * FF_CHUNK)
            h = jnp.maximum(jnp.dot(yb, wu_ref[:, cols], preferred_element_type=F32), 0.0)
            acc = acc + jnp.dot((h * h).astype(BF16), wd_ref[cols, :],
                                preferred_element_type=F32)
        o_ref[rows, :] = _ln(ALPHA * y + acc, g2, b2)


def _out_ffn(x, a, ma, wo_bf, g1, b1, wu_bf, wd_bf, g2, b2, layer, tm, name):
    m = x.shape[0]
    row = lambda i: (i, 0)
    resident = lambda arr: pl.BlockSpec((None,) + arr.shape[1:], lambda i: (layer, 0, 0),
                                        pipeline_mode=pl.Buffered(1))
    ln = lambda arr: pl.BlockSpec(arr.shape, lambda i: (0, 0), pipeline_mode=pl.Buffered(1))
    return pl.pallas_call(
        functools.partial(_out_ffn_body, layer=layer),
        grid=(m // tm,),
        in_specs=[
            pl.BlockSpec((tm, D_MODEL), row),
            pl.BlockSpec((tm, SELF_WIDTH), row),
            pl.BlockSpec((tm, MEM_WIDTH), row),
            resident(wo_bf), ln(g1), ln(b1),
            resident(wu_bf), resident(wd_bf), ln(g2), ln(b2),
        ],
        out_specs=pl.BlockSpec((tm, D_MODEL), row),
        out_shape=jax.ShapeDtypeStruct((m, D_MODEL), F32),
        compiler_params=_params("arbitrary"),
        name=name,
    )(x, a, ma, wo_bf, g1, b1, wu_bf, wd_bf, g2, b2)


def kernel(x_prompt, x_sample, cache_k, cache_v, cache_mem_k, cache_mem_v, page_table, mem_prompt, w_in_attn, lambda_qk, subln_g, w_in_sgu, sgu_ln_g, sgu_ln_b, sgu_w, sgu_b, w_mem_kv, w_out, ln1_g, ln1_b, w_up, w_down, ln2_g, ln2_b):
    nb, seq, _ = x_prompt.shape
    db = x_sample.shape[0]
    row = lambda a: a.reshape(1, -1)

    xp = x_prompt.reshape(nb * seq, D_MODEL)
    xs = x_sample.reshape(db, D_MODEL)
    w_in_attn_bf = w_in_attn.astype(BF16)
    cache_k_pg = jnp.transpose(cache_k[0], (0, 2, 1, 3))
    cache_v_pg = jnp.transpose(cache_v[0], (0, 2, 1, 3))
    mem_kt = jnp.transpose(cache_mem_k, (0, 1, 3, 4, 2)).reshape(DEPTH, db, MEM_WIDTH, MEM_LEN)
    mem_vt = jnp.transpose(cache_mem_v, (0, 1, 3, 4, 2)).reshape(DEPTH, db, MEM_WIDTH, MEM_LEN)

    mkt_p, mvt_p, mk_bf, mvm_bf = _mem_kv(mem_prompt, w_mem_kv.astype(BF16))

    lam0 = _lambda_init(0)
    g0 = row(subln_g[0])
    flat = lambda w: w.reshape(-1, w.shape[-1])
    (q_bf, k_pg, v_pg, k_bf, vt_bf, ma_p, w_out_2d, w_up_2d, w_down_2d, w_sgu_2d) = _attn_proj_prompt(
        xp, w_in_attn_bf[0], mk_bf, mvm_bf, 0, seq,
        [flat(w_out), flat(w_up), flat(w_down), flat(w_in_sgu)])
    w_out_bf = w_out_2d.reshape(w_out.shape)
    w_up_bf = w_up_2d.reshape(w_up.shape)
    w_down_bf = w_down_2d.reshape(w_down.shape)
    w_in_sgu_bf = w_sgu_2d.reshape(w_in_sgu.shape)

    def out_ffn(x, a, ma, l, tm, name):
        return _out_ffn(x, a, ma, w_out_bf, ln1_g, ln1_b, w_up_bf, w_down_bf, ln2_g, ln2_b,
                        l, tm, name)

    o_p = _diff_attn_prompt(q_bf, k_bf, vt_bf, lambda_qk[0], g0, lam0)
    xp = out_ffn(xp, o_p.reshape(nb * seq, SELF_WIDTH), ma_p, 0, FFN_TOKEN_TILE, "out_ffn_prompt0")

    q_s, k_s, v_s, qm_s = _attn_proj_sample(xs, w_in_attn_bf[0])
    o_s, ma_s = _diff_attn_decode(q_s, k_s, v_s, qm_s, mem_kt, mem_vt, 0, cache_k_pg, cache_v_pg,
                                  page_table, lambda_qk[0], g0, lam0)
    xs = out_ffn(xs, o_s, ma_s, 0, db, "out_ffn_sample0")

    sgu_b_full = jnp.broadcast_to(sgu_b[0][:, :, None], (SGU_GROUPS, CHUNK, CHUNK))
    a_p, ma_p = _sgu_prompt(xp, w_in_sgu_bf[0], row(sgu_ln_g[0]), row(sgu_ln_b[0]), sgu_w[0],
                            sgu_b_full, mk_bf, mvm_bf, 1, seq)
    xp = out_ffn(xp, a_p, ma_p, 1, FFN_TOKEN_TILE, "out_ffn_prompt1")

    a_s, qm_s, sv_s = _sgu_sample(xs, w_in_sgu_bf[0], row(sgu_ln_g[0]), row(sgu_ln_b[0]),
                                  sgu_w[0], sgu_b[0])
    ma_s = _mem_decode(qm_s, mem_kt, mem_vt, 1)
    xs = out_ffn(xs, a_s, ma_s, 1, db, "out_ffn_sample1")

    n_pg = seq // PAGE_SIZE
    paged = lambda a: jnp.transpose(
        a.reshape(1, nb, n_pg, DIFF_HEADS, PAGE_SIZE, HEAD_WIDTH), (0, 1, 2, 4, 3, 5))
    mem_out = lambda a: jnp.transpose(
        a.reshape(DEPTH, nb, MEM_HEADS, MEM_HEAD_DIM, MEM_LEN), (0, 1, 4, 2, 3))
    dec_shape = (1, db, 1, DIFF_HEADS, HEAD_WIDTH)
    return (xp.reshape(nb, seq, D_MODEL), xs.reshape(db, 1, D_MODEL),
            paged(k_pg), paged(v_pg), mem_out(mkt_p), mem_out(mvt_p),
            k_s.reshape(dec_shape), v_s.reshape(dec_shape), sv_s.reshape(dec_shape))
```
